```python
import jax, jax.numpy as jnp
from jax import lax
import numpy as np

D_MODEL = 4096
BATCH = 1
SEQ = 16384
DEPTH = 2
DEC_BATCH = 4
DEC_SEQ = 2048
PAST_LEN = 128

HEAD_DIM = 128
MIX_WIDTH = D_MODEL
H_A = MIX_WIDTH // 4 // HEAD_DIM
H_B = MIX_WIDTH // 2 // HEAD_DIM
HKV_B = H_B // 4
H_C = MIX_WIDTH // 4 // HEAD_DIM
DILATED_PATTERNS = ((128, 1), (512, 4), (2048, 16))
ROPE_THETA = 500000.0
ROPE_DIM = HEAD_DIM // 4
AXIAL_THETA = 10000.0
Q_BLOCK = 128
GRID_W = 64
NA_ROWS = 8
NA_COLS = 16
D_FF = -((-8 * D_MODEL) // (3 * 256)) * 256
DEEPNORM_ALPHA = (2.0 * DEPTH) ** 0.25
DEEPNORM_BETA = (8.0 * DEPTH) ** -0.25
LN_EPS = 1e-5
RMS_EPS = 1e-6
NEG_INF = -1e30
IN_SIZES = (H_A * HEAD_DIM, H_A * HEAD_DIM, H_A * HEAD_DIM,
            H_B * HEAD_DIM, HKV_B * HEAD_DIM, HKV_B * HEAD_DIM,
            H_C * HEAD_DIM, H_C * HEAD_DIM, H_C * HEAD_DIM)

kernel_name = 'hybrid_dilated_axial_neighbourhood_encoder'


def layer_norm(x, g, b):
    xf = x.astype(jnp.float32)
    mu = jnp.mean(xf, -1, keepdims=True)
    var = jnp.mean(jnp.square(xf - mu), -1, keepdims=True)
    return ((xf - mu) * lax.rsqrt(var + LN_EPS) * g.astype(jnp.float32) + b.astype(jnp.float32)).astype(x.dtype)


def rms_norm(x, g):
    xf = x.astype(jnp.float32)
    return (xf * lax.rsqrt(jnp.mean(jnp.square(xf), -1, keepdims=True) + RMS_EPS) * g.astype(jnp.float32)).astype(x.dtype)


def group_rms(o):
    b, s, h, dh = o.shape
    of = o.reshape(b, s, h * dh).astype(jnp.float32)
    return (of * lax.rsqrt(jnp.mean(jnp.square(of), -1, keepdims=True) + RMS_EPS)).astype(o.dtype)


def rope(x, pos, theta):
    d = x.shape[-1]
    half = d // 2
    inv = theta ** (-jnp.arange(half, dtype=jnp.float32) * 2.0 / d)
    ang = pos.astype(jnp.float32)[:, None] * inv[None, :]
    cos = jnp.cos(ang)[:, None, :].astype(x.dtype)
    sin = jnp.sin(ang)[:, None, :].astype(x.dtype)
    x1, x2 = x[..., :half], x[..., half:]
    return jnp.concatenate([x1 * cos - x2 * sin, x2 * cos + x1 * sin], -1)


def banded_attention(q, k, v, half):
    n, length, dh = q.shape
    blk = half
    nb = -(-length // blk)
    lp = nb * blk
    qb = jnp.pad(q, ((0, 0), (0, lp - length), (0, 0))).reshape(n, nb, blk, dh)

    def windows(a):
        ab = jnp.pad(a, ((0, 0), (blk, lp - length + blk), (0, 0))).reshape(n, nb + 2, blk, dh)
        return jnp.concatenate([ab[:, :-2], ab[:, 1:-1], ab[:, 2:]], axis=2)

    kw, vw = windows(k), windows(v)
    s = jnp.einsum('nbqd,nbkd->nbqk', qb, kw).astype(jnp.float32) * (dh ** -0.5)
    qpos = jnp.arange(nb)[:, None] * blk + jnp.arange(blk)[None, :]
    kpos = jnp.arange(nb)[:, None] * blk - blk + jnp.arange(3 * blk)[None, :]
    rel = kpos[:, None, :] - qpos[:, :, None]
    valid = (jnp.abs(rel) <= half) & (kpos[:, None, :] >= 0) & (kpos[:, None, :] < length)
    s = jnp.where(valid, s, NEG_INF)
    lse = jax.nn.logsumexp(s, -1)
    p = jnp.exp(s - lse[..., None]).astype(v.dtype)
    o = jnp.einsum('nbqk,nbkd->nbqd', p, vw)
    return o.reshape(n, lp, dh)[:, :length], lse.reshape(n, lp)[:, :length]


def dilated_attention(q, k, v):
    b, s, h, dh = q.shape
    t = jnp.arange(s)
    q = jnp.concatenate([rope(q[..., :ROPE_DIM], t, ROPE_THETA), q[..., ROPE_DIM:]], -1)
    k = jnp.concatenate([rope(k[..., :ROPE_DIM], t, ROPE_THETA), k[..., ROPE_DIM:]], -1)
    outs, lses = [], []
    for window, dil in DILATED_PATTERNS:
        n = s // dil
        qr = q.reshape(b, n, dil, h, dh).transpose(0, 2, 3, 1, 4).reshape(b * dil * h, n, dh)
        kr = k.reshape(b, n, dil, h, dh).transpose(0, 2, 3, 1, 4).reshape(b * dil * h, n, dh)
        vr = v.reshape(b, n, dil, h, dh).transpose(0, 2, 3, 1, 4).reshape(b * dil * h, n, dh)
        o, lse = banded_attention(qr, kr, vr, (window // 2) // dil)
        outs.append(o.reshape(b, dil, h, n, dh).transpose(0, 3, 1, 2, 4).reshape(b, s, h, dh))
        lses.append(lse.reshape(b, dil, h, n).transpose(0, 3, 1, 2).reshape(b, s, h))
    wts = jax.nn.softmax(jnp.stack(lses, -1), -1)
    return jnp.einsum('bshpd,bshp->bshd', jnp.stack(outs, 3), wts.astype(q.dtype))


def axial_gqa(q, k, v, g_q, g_k):
    b, s, h, dh = q.shape
    hkv = k.shape[2]
    grp = h // hkv
    t = jnp.arange(s)
    row, col = t // GRID_W, t % GRID_W
    half = dh // 2
    q = rms_norm(q, g_q)
    k = rms_norm(k, g_k)
    q = jnp.concatenate([rope(q[..., :half], row, AXIAL_THETA), rope(q[..., half:], col, AXIAL_THETA)], -1)
    k = jnp.concatenate([rope(k[..., :half], row, AXIAL_THETA), rope(k[..., half:], col, AXIAL_THETA)], -1)
    nblk = s // Q_BLOCK
    qb = q.reshape(b, nblk, Q_BLOCK, hkv, grp, dh).transpose(1, 0, 2, 3, 4, 5)
    scale = dh ** -0.5

    def block(qi):
        sc = jnp.einsum('bqkgd,bskd->bkgqs', qi, k).astype(jnp.float32) * scale
        p = jax.nn.softmax(sc, -1).astype(v.dtype)
        return jnp.einsum('bkgqs,bskd->bqkgd', p, v)

    o = lax.map(block, qb)
    return o.transpose(1, 0, 2, 3, 4, 5).reshape(b, s, h, dh)


def neighbourhood_attention(q, k, v, rpb):
    b, s, h, dh = q.shape
    rows = s // GRID_W
    kr = min(NA_ROWS, rows)
    kc = NA_COLS
    qg = q.reshape(b, rows, GRID_W, h, dh).transpose(0, 3, 1, 2, 4)
    kg = k.reshape(b, rows, GRID_W, h, dh).transpose(0, 3, 1, 2, 4)
    vg = v.reshape(b, rows, GRID_W, h, dh).transpose(0, 3, 1, 2, 4)
    r = jnp.arange(rows)
    c = jnp.arange(GRID_W)
    rstart = jnp.clip(r - kr // 2, 0, rows - kr)
    ridx = rstart[:, None] + jnp.arange(kr)[None, :]
    kn, vn = kg[:, :, ridx], vg[:, :, ridx]
    sc = jnp.einsum('bhiqd,bhirkd->bhiqrk', qg, kn).astype(jnp.float32) * (dh ** -0.5)
    roff = ridx - r[:, None] + (NA_ROWS - 1)
    coff = jnp.clip(c[None, :] - c[:, None], -(kc - 1), kc - 1) + (kc - 1)
    bias = rpb[:, roff[:, None, :, None], coff[None, :, None, :]]
    cstart = jnp.clip(c - kc // 2, 0, GRID_W - kc)
    cmask = (c[None, :] >= cstart[:, None]) & (c[None, :] < cstart[:, None] + kc)
    sc = jnp.where(cmask[:, None, :], sc + bias.astype(jnp.float32), NEG_INF)
    p = jax.nn.softmax(sc.reshape(b, h, rows, GRID_W, kr * GRID_W), -1).reshape(sc.shape).astype(v.dtype)
    o = jnp.einsum('bhiqrk,bhirkd->bhiqd', p, vn)
    return o.transpose(0, 2, 3, 1, 4).reshape(b, s, h, dh)


def encoder_layer(x, w_in, g_qn, g_kn, rpb, g_grp, w_out, ln1_g, ln1_b, w_gate, w_up, w_down, ln2_g, ln2_b):
    b, s, _ = x.shape
    proj = jnp.einsum('bsd,de->bse', x, w_in)
    offsets = np.cumsum(IN_SIZES)[:-1].tolist()
    qa, ka, va, qb, kb, vb, qc, kc, vc = jnp.split(proj, offsets, axis=-1)
    oa = dilated_attention(qa.reshape(b, s, H_A, HEAD_DIM), ka.reshape(b, s, H_A, HEAD_DIM),
                           va.reshape(b, s, H_A, HEAD_DIM))
    ob = axial_gqa(qb.reshape(b, s, H_B, HEAD_DIM), kb.reshape(b, s, HKV_B, HEAD_DIM),
                   vb.reshape(b, s, HKV_B, HEAD_DIM), g_qn, g_kn)
    oc = neighbourhood_attention(qc.reshape(b, s, H_C, HEAD_DIM), kc.reshape(b, s, H_C, HEAD_DIM),
                                 vc.reshape(b, s, H_C, HEAD_DIM), rpb)
    mixed = jnp.concatenate([group_rms(oa), group_rms(ob), group_rms(oc)], -1) * g_grp
    x = layer_norm(DEEPNORM_ALPHA * x + jnp.einsum('bse,ed->bsd', mixed, w_out), ln1_g, ln1_b)
    hdn = jax.nn.silu(jnp.einsum('bsd,df->bsf', x, w_gate)) * jnp.einsum('bsd,df->bsf', x, w_up)
    x = layer_norm(DEEPNORM_ALPHA * x + jnp.einsum('bsf,fd->bsd', hdn, w_down), ln2_g, ln2_b)
    return x


def setup_inputs(seed: int = 0) -> dict:
    key = jax.random.key(seed)
    ks = jax.random.split(key, 16)

    def nrm(k, shape, scale):
        return jax.random.normal(k, shape, jnp.float32) * scale

    in_width = sum(IN_SIZES)
    return {
        'x_prompt': nrm(ks[0], (BATCH, SEQ, D_MODEL), 1.0),
        'x_sample': nrm(ks[1], (DEC_BATCH, DEC_SEQ, D_MODEL), 1.0),
        'w_in': nrm(ks[2], (DEPTH, D_MODEL, in_width), D_MODEL ** -0.5),
        'g_qn': 1.0 + nrm(ks[3], (DEPTH, HEAD_DIM), 0.02),
        'g_kn': 1.0 + nrm(ks[4], (DEPTH, HEAD_DIM), 0.02),
        'rpb': nrm(ks[5], (DEPTH, H_C, 2 * NA_ROWS - 1, 2 * NA_COLS - 1), 0.02),
        'g_grp': 1.0 + nrm(ks[6], (DEPTH, MIX_WIDTH), 0.02),
        'w_out': nrm(ks[7], (DEPTH, MIX_WIDTH, D_MODEL), MIX_WIDTH ** -0.5 * DEEPNORM_BETA),
        'ln1_g': 1.0 + nrm(ks[8], (DEPTH, D_MODEL), 0.02),
        'ln1_b': nrm(ks[9], (DEPTH, D_MODEL), 0.02),
        'w_gate': nrm(ks[10], (DEPTH, D_MODEL, D_FF), D_MODEL ** -0.5),
        'w_up': nrm(ks[11], (DEPTH, D_MODEL, D_FF), D_MODEL ** -0.5),
        'w_down': nrm(ks[12], (DEPTH, D_FF, D_MODEL), D_FF ** -0.5 * DEEPNORM_BETA),
        'ln2_g': 1.0 + nrm(ks[13], (DEPTH, D_MODEL), 0.02),
        'ln2_b': nrm(ks[14], (DEPTH, D_MODEL), 0.02),
    }


def reference(x_prompt, x_sample, w_in, g_qn, g_kn, rpb, g_grp, w_out, ln1_g, ln1_b, w_gate, w_up, w_down, ln2_g, ln2_b):
    def trunk(x):
        for l in range(DEPTH):
            x = encoder_layer(x, w_in[l], g_qn[l], g_kn[l], rpb[l], g_grp[l], w_out[l],
                              ln1_g[l], ln1_b[l], w_gate[l], w_up[l], w_down[l], ln2_g[l], ln2_b[l])
        return x

    y_prompt = trunk(x_prompt)
    y_sample = trunk(x_sample)
    return (y_prompt, y_sample)
```

```python
import functools

import numpy as np
import jax
import jax.numpy as jnp
from jax import lax
from jax.experimental import pallas as pl
from jax.experimental.pallas import tpu as pltpu

F32 = jnp.float32
BF16 = jnp.bfloat16

HEAD_DIM = 128
H_A, H_B, HKV_B, H_C = 8, 16, 4, 8
GRP_B = H_B // HKV_B
DILATED_PATTERNS = ((128, 1), (512, 4), (2048, 16))
ROPE_THETA = 500000.0
ROPE_DIM = HEAD_DIM // 4
AXIAL_THETA = 10000.0
GRID_W = 64
NA_ROWS = 8
NA_COLS = 16
LN_EPS = 1e-5
RMS_EPS = 1e-6
NEG_INF = -1e30
DEPTH = 2
DEEPNORM_ALPHA = (2.0 * DEPTH) ** 0.25

W_A = H_A * HEAD_DIM
W_BQ = H_B * HEAD_DIM
W_BKV = HKV_B * HEAD_DIM
W_C = H_C * HEAD_DIM
OFF_QA, OFF_KA, OFF_VA = 0, W_A, 2 * W_A
OFF_QB = 3 * W_A
OFF_KB = OFF_QB + W_BQ
OFF_VB = OFF_KB + W_BKV
OFF_QC = OFF_VB + W_BKV
OFF_KC = OFF_QC + W_C
OFF_VC = OFF_KC + W_C
IN_WIDTH = OFF_VC + W_C

LANES = 128
VMEM_LIMIT_BYTES = 52 * 1024 * 1024

BAND_REACH = max(w // 2 for w, _ in DILATED_PATTERNS)
BAND_TILE = 256
NA_QROWS = 4
NA_TILE = NA_QROWS * GRID_W
NA_KROWS = 12


def _params(sem):
    return pltpu.CompilerParams(dimension_semantics=sem, vmem_limit_bytes=VMEM_LIMIT_BYTES)


def _rot(h, c, s_up, s_dn, shift):
    return h * c + pltpu.roll(h, LANES - shift, 1) * s_up + pltpu.roll(h, shift, 1) * s_dn


def _inproj_kernel(x_ref, w_ref, ca_ref, ua_ref, da_ref, cb_ref, ub_ref, db_ref, gq_ref, gk_ref,
                   o_ref, acc_ref, *, tn):
    j = pl.program_id(1)
    acc_ref[...] = jnp.dot(x_ref[...], w_ref[...], preferred_element_type=F32)
    heads = tn // HEAD_DIM
    is_rope_a = j < OFF_VA // tn
    is_qb = (j >= OFF_QB // tn) & (j < OFF_KB // tn)
    is_kb = (j >= OFF_KB // tn) & (j < OFF_VB // tn)
    is_plain = jnp.logical_not(is_rope_a | is_qb | is_kb)

    @pl.when(is_plain)
    def _():
        o_ref[...] = acc_ref[...].astype(o_ref.dtype)

    @pl.when(is_rope_a)
    def _():
        for hh in range(heads):
            sl = slice(hh * HEAD_DIM, (hh + 1) * HEAD_DIM)
            o_ref[:, sl] = _rot(acc_ref[:, sl], ca_ref[...], ua_ref[...], da_ref[...],
                                ROPE_DIM // 2).astype(o_ref.dtype)

    def norm_rope_b(g_ref):
        for hh in range(heads):
            sl = slice(hh * HEAD_DIM, (hh + 1) * HEAD_DIM)
            h = acc_ref[:, sl]
            h = h * lax.rsqrt(jnp.mean(h * h, axis=-1, keepdims=True) + RMS_EPS) * g_ref[...]
            o_ref[:, sl] = _rot(h, cb_ref[...], ub_ref[...], db_ref[...],
                                HEAD_DIM // 4).astype(o_ref.dtype)

    @pl.when(is_qb)
    def _():
        norm_rope_b(gq_ref)

    @pl.when(is_kb)
    def _():
        norm_rope_b(gk_ref)


def _in_proj(xb, w, tabs, gq, gk, seq):
    m, k = xb.shape
    n = w.shape[1]
    tm, tn = 1024, 512
    assert m % tm == 0 and n % tn == 0 and seq % tm == 0
    for off in (OFF_VA, OFF_QB, OFF_KB, OFF_VB):
        assert off % tn == 0
    nseq = seq // tm
    tab_spec = pl.BlockSpec((tm, LANES), lambda i, j: (i % nseq, 0))
    g_spec = pl.BlockSpec((1, LANES), lambda i, j: (0, 0))
    return pl.pallas_call(
        functools.partial(_inproj_kernel, tn=tn),
        grid=(m // tm, n // tn),
        in_specs=[pl.BlockSpec((tm, k), lambda i, j: (i, 0)),
                  pl.BlockSpec((k, tn), lambda i, j: (0, j))] + [tab_spec] * 6 + [g_spec, g_spec],
        out_specs=pl.BlockSpec((tm, tn), lambda i, j: (i, j)),
        out_shape=jax.ShapeDtypeStruct((m, n), BF16),
        scratch_shapes=[pltpu.VMEM((tm, tn), F32)],
        compiler_params=_params(("parallel", "arbitrary")),
        name="in_proj",
    )(xb, w, *tabs, gq, gk)


def _rope_tables(seq):
    def cs(pos, theta, d):
        half = d // 2
        inv = theta ** (-jnp.arange(half, dtype=F32) * 2.0 / d)
        ang = pos.astype(F32)[:, None] * inv[None, :]
        return jnp.cos(ang), jnp.sin(ang)

    t = jnp.arange(seq)
    ca, sa = cs(t, ROPE_THETA, ROPE_DIM)
    pad = HEAD_DIM - ROPE_DIM
    z = lambda w_: jnp.zeros((seq, w_), F32)
    tab_ca = jnp.concatenate([ca, ca, jnp.ones((seq, pad), F32)], 1)
    tab_ua = jnp.concatenate([-sa, z(ROPE_DIM // 2 + pad)], 1)
    tab_da = jnp.concatenate([z(ROPE_DIM // 2), sa, z(pad)], 1)
    cr, sr = cs(t // GRID_W, AXIAL_THETA, HEAD_DIM // 2)
    cc, sc = cs(t % GRID_W, AXIAL_THETA, HEAD_DIM // 2)
    q = HEAD_DIM // 4
    tab_cb = jnp.concatenate([cr, cr, cc, cc], 1)
    tab_ub = jnp.concatenate([-sr, z(q), -sc, z(q)], 1)
    tab_db = jnp.concatenate([z(q), sr, z(q), sc], 1)
    return (tab_ca, tab_ua, tab_da, tab_cb, tab_ub, tab_db)


def _mm_resid_kernel(x_ref, w_ref, r_ref, o_ref):
    o_ref[...] = DEEPNORM_ALPHA * r_ref[...] + jnp.dot(x_ref[...], w_ref[...], preferred_element_type=F32)


def _mm_resid(xb, w, resid, tm, tn):
    m, k = xb.shape
    n = w.shape[1]
    assert m % tm == 0 and n % tn == 0
    return pl.pallas_call(
        _mm_resid_kernel,
        grid=(m // tm, n // tn),
        in_specs=[pl.BlockSpec((tm, k), lambda i, j: (i, 0)),
                  pl.BlockSpec((k, tn), lambda i, j: (0, j)),
                  pl.BlockSpec((tm, tn), lambda i, j: (i, j))],
        out_specs=pl.BlockSpec((tm, tn), lambda i, j: (i, j)),
        out_shape=jax.ShapeDtypeStruct((m, n), F32),
        compiler_params=_params(("parallel", "arbitrary")),
        name="mm_resid",
    )(xb, w, resid)


def _gateup_kernel(x_ref, wg_ref, wu_ref, o_ref):
    x = x_ref[...]
    g = jnp.dot(x, wg_ref[...], preferred_element_type=F32)
    u = jnp.dot(x, wu_ref[...], preferred_element_type=F32)
    o_ref[...] = (g / (1.0 + jnp.exp(-g)) * u).astype(o_ref.dtype)


def _gate_up(xb, wg, wu, tm, tn):
    m, k = xb.shape
    n = wg.shape[1]
    assert m % tm == 0 and n % tn == 0
    return pl.pallas_call(
        _gateup_kernel,
        grid=(m // tm, n // tn),
        in_specs=[pl.BlockSpec((tm, k), lambda i, j: (i, 0)),
                  pl.BlockSpec((k, tn), lambda i, j: (0, j)),
                  pl.BlockSpec((k, tn), lambda i, j: (0, j))],
        out_specs=pl.BlockSpec((tm, tn), lambda i, j: (i, j)),
        out_shape=jax.ShapeDtypeStruct((m, n), BF16),
        compiler_params=_params(("parallel", "arbitrary")),
        name="gate_up",
    )(xb, wg, wu)


def _ln_kernel(y_ref, g_ref, b_ref, o_ref, ob_ref):
    y = y_ref[...]
    mu = jnp.mean(y, axis=-1, keepdims=True)
    d = y - mu
    var = jnp.mean(d * d, axis=-1, keepdims=True)
    o = d * lax.rsqrt(var + LN_EPS) * g_ref[...] + b_ref[...]
    o_ref[...] = o
    ob_ref[...] = o.astype(ob_ref.dtype)


def _layer_norm(y, g, b, tm=256):
    m, d = y.shape
    row = pl.BlockSpec((tm, d), lambda i: (i, 0))
    vec = pl.BlockSpec((1, d), lambda i: (0, 0))
    return pl.pallas_call(
        _ln_kernel,
        grid=(m // tm,),
        in_specs=[row, vec, vec],
        out_specs=[row, row],
        out_shape=[jax.ShapeDtypeStruct((m, d), F32), jax.ShapeDtypeStruct((m, d), BF16)],
        compiler_params=_params(("parallel",)),
        name="layer_norm",
    )(y, g.reshape(1, d), b.reshape(1, d))


def _mix_kernel(oa_ref, ob_ref, oc_ref, g_ref, o_ref):
    off = 0
    for ref in (oa_ref, ob_ref, oc_ref):
        w = ref.shape[1]
        x = ref[...].astype(F32)
        xn = x * lax.rsqrt(jnp.mean(x * x, axis=-1, keepdims=True) + RMS_EPS)
        o_ref[:, off:off + w] = (xn * g_ref[:, off:off + w]).astype(o_ref.dtype)
        off += w


def _mix(oa, ob, oc, g, tm=512):
    m = oa.shape[0]
    d = oa.shape[1] + ob.shape[1] + oc.shape[1]
    return pl.pallas_call(
        _mix_kernel,
        grid=(m // tm,),
        in_specs=[pl.BlockSpec((tm, oa.shape[1]), lambda i: (i, 0)),
                  pl.BlockSpec((tm, ob.shape[1]), lambda i: (i, 0)),
                  pl.BlockSpec((tm, oc.shape[1]), lambda i: (i, 0)),
                  pl.BlockSpec((1, d), lambda i: (0, 0))],
        out_specs=pl.BlockSpec((tm, d), lambda i: (i, 0)),
        out_shape=jax.ShapeDtypeStruct((m, d), BF16),
        compiler_params=_params(("parallel",)),
        name="group_rms_mix",
    )(oa, ob, oc, g.reshape(1, d))


def _attn_b_kernel(q_ref, k_ref, v_ref, o_ref, m_ref, l_ref, acc_ref, *, scale):
    kv = pl.program_id(3)
    tq = q_ref.shape[0]

    @pl.when(kv == 0)
    def _():
        m_ref[...] = jnp.full(m_ref.shape, -jnp.inf, F32)
        l_ref[...] = jnp.zeros(l_ref.shape, F32)
        acc_ref[...] = jnp.zeros(acc_ref.shape, F32)

    q = jnp.concatenate([q_ref[:, r * HEAD_DIM:(r + 1) * HEAD_DIM] for r in range(GRP_B)], axis=0)
    s = lax.dot_general(q, k_ref[...], (((1,), (1,)), ((), ())), preferred_element_type=F32) * scale
    m_prev = m_ref[...]
    m_new = jnp.maximum(m_prev, jnp.max(s, axis=-1, keepdims=True))
    alpha = jnp.exp(m_prev - m_new)
    p = jnp.exp(s - m_new)
    l_ref[...] = alpha * l_ref[...] + jnp.sum(p, axis=-1, keepdims=True)
    acc_ref[...] = alpha * acc_ref[...] + jnp.dot(p.astype(v_ref.dtype), v_ref[...],
                                                  preferred_element_type=F32)
    m_ref[...] = m_new

    @pl.when(kv == pl.num_programs(3) - 1)
    def _():
        o = acc_ref[...] / l_ref[...]
        for r in range(GRP_B):
            o_ref[:, r * HEAD_DIM:(r + 1) * HEAD_DIM] = o[r * tq:(r + 1) * tq].astype(o_ref.dtype)


def _attn_b(proj3, tq=256, tk=512):
    b, s, _ = proj3.shape
    qw = GRP_B * HEAD_DIM
    assert s % tq == 0 and s % tk == 0 and OFF_QB % qw == 0
    return pl.pallas_call(
        functools.partial(_attn_b_kernel, scale=HEAD_DIM ** -0.5),
        grid=(b, HKV_B, s // tq, s // tk),
        in_specs=[pl.BlockSpec((None, tq, qw), lambda bi, g, qi, ki: (bi, qi, OFF_QB // qw + g)),
                  pl.BlockSpec((None, tk, HEAD_DIM), lambda bi, g, qi, ki: (bi, ki, OFF_KB // HEAD_DIM + g)),
                  pl.BlockSpec((None, tk, HEAD_DIM), lambda bi, g, qi, ki: (bi, ki, OFF_VB // HEAD_DIM + g))],
        out_specs=pl.BlockSpec((None, tq, qw), lambda bi, g, qi, ki: (bi, qi, g)),
        out_shape=jax.ShapeDtypeStruct((b, s, W_BQ), BF16),
        scratch_shapes=[pltpu.VMEM((GRP_B * tq, 1), F32), pltpu.VMEM((GRP_B * tq, 1), F32),
                        pltpu.VMEM((GRP_B * tq, HEAD_DIM), F32)],
        compiler_params=_params(("parallel", "parallel", "parallel", "arbitrary")),
        name="attn_axial_gqa",
    )(proj3, proj3, proj3)


def _band_bias_table():
    nt = 2 * BAND_REACH // BAND_TILE + 1
    u = np.arange(nt)[:, None, None] - BAND_REACH // BAND_TILE
    rel = u * BAND_TILE + np.arange(BAND_TILE)[None, None, :] - np.arange(BAND_TILE)[None, :, None]
    mult = np.zeros(rel.shape, np.int64)
    for window, dil in DILATED_PATTERNS:
        mult += (np.abs(rel) <= window // 2) & (rel % dil == 0)
    return np.where(mult > 0, np.log(np.maximum(mult, 1)), NEG_INF).astype(np.float32)


def _attn_a_kernel(q_ref, k_ref, v_ref, tbl_ref, o_ref, *, scale, seq):
    t = BAND_TILE
    qi = pl.program_id(2)
    t0 = qi * t
    q = q_ref[...]
    jlo = jnp.maximum(t0 - BAND_REACH, 0) // t
    jhi = jnp.minimum(t0 + t - 1 + BAND_REACH, seq - 1) // t + 1

    def body(j, carry):
        m_prev, l_prev, acc = carry
        start = pl.multiple_of(j * t, t)
        kb = k_ref[pl.ds(start, t), :]
        vb = v_ref[pl.ds(start, t), :]
        s = lax.dot_general(q, kb, (((1,), (1,)), ((), ())), preferred_element_type=F32) * scale
        s = s + tbl_ref[j - qi + BAND_REACH // t]
        m_new = jnp.maximum(m_prev, jnp.max(s, axis=-1, keepdims=True))
        alpha = jnp.exp(m_prev - m_new)
        p = jnp.exp(s - m_new)
        l_new = alpha * l_prev + jnp.sum(p, axis=-1, keepdims=True)
        acc = alpha * acc + jnp.dot(p.astype(vb.dtype), vb, preferred_element_type=F32)
        return m_new, l_new, acc

    init = (jnp.full((t, 1), -jnp.inf, F32), jnp.zeros((t, 1), F32), jnp.zeros((t, HEAD_DIM), F32))
    _, l, acc = lax.fori_loop(jlo, jhi, body, init)
    o_ref[...] = (acc / l).astype(o_ref.dtype)


def _attn_a(proj3, band_tbl):
    b, s, _ = proj3.shape
    t = BAND_TILE
    assert s % t == 0
    nt = band_tbl.shape[0]
    kv_spec = lambda off: pl.BlockSpec((None, s, HEAD_DIM), lambda bi, h, qi: (bi, 0, off // HEAD_DIM + h))
    return pl.pallas_call(
        functools.partial(_attn_a_kernel, scale=HEAD_DIM ** -0.5, seq=s),
        grid=(b, H_A, s // t),
        in_specs=[pl.BlockSpec((None, t, HEAD_DIM), lambda bi, h, qi: (bi, qi, OFF_QA // HEAD_DIM + h)),
                  kv_spec(OFF_KA), kv_spec(OFF_VA),
                  pl.BlockSpec((nt, t, t), lambda bi, h, qi: (0, 0, 0))],
        out_specs=pl.BlockSpec((None, t, HEAD_DIM), lambda bi, h, qi: (bi, qi, h)),
        out_shape=jax.ShapeDtypeStruct((b, s, W_A), BF16),
        compiler_params=_params(("parallel", "parallel", "arbitrary")),
        name="attn_dilated",
    )(proj3, proj3, proj3, band_tbl)


def _rpb_table_kernel(rpb_ref, o_ref):
    h = pl.program_id(0)
    a = pl.program_id(1) - 1
    n_roff, n_coff = 2 * NA_ROWS - 1, 2 * NA_COLS - 1
    shape = (GRID_W, 2 * GRID_W)
    qc = lax.broadcasted_iota(jnp.int32, shape, 0)
    lane = lax.broadcasted_iota(jnp.int32, shape, 1)
    hi = lane >= GRID_W
    kc = jnp.where(hi, lane - GRID_W, lane)
    coff = jnp.clip(kc - qc, -(NA_COLS - 1), NA_COLS - 1) + (NA_COLS - 1)
    cstart = jnp.clip(qc - NA_COLS // 2, 0, GRID_W - NA_COLS)
    cmask = (kc >= cstart) & (kc < cstart + NA_COLS)
    a_lo = jnp.clip(a, 0, n_roff - 1)
    a_hi = jnp.clip(a + 1, 0, n_roff - 1)
    val = jnp.zeros(shape, F32)
    for bb in range(n_coff):
        r_lo = rpb_ref[(h * n_roff + a_lo) * n_coff + bb]
        r_hi = rpb_ref[(h * n_roff + a_hi) * n_coff + bb]
        val = jnp.where(coff == bb, jnp.where(hi, r_hi, r_lo), val)
    ok_lo = ((a >= 0) & (a < n_roff)).astype(jnp.int32)
    ok_hi = ((a + 1 >= 0) & (a + 1 < n_roff)).astype(jnp.int32)
    ok = jnp.where(hi, ok_hi, ok_lo) > 0
    o_ref[0, 0] = jnp.where(cmask & ok, val, NEG_INF)


def _rpb_table(rpb):
    h = rpb.shape[0]
    n_slots = 2 * NA_ROWS
    return pl.pallas_call(
        _rpb_table_kernel,
        grid=(h, n_slots),
        in_specs=[pl.BlockSpec(memory_space=pltpu.SMEM)],
        out_specs=pl.BlockSpec((1, 1, GRID_W, 2 * GRID_W), lambda hi, ai: (hi, ai, 0, 0)),
        out_shape=jax.ShapeDtypeStruct((h, n_slots, GRID_W, 2 * GRID_W), F32),
        compiler_params=_params(("arbitrary", "arbitrary")),
        name="rpb_table",
    )(rpb.reshape(-1))


def _attn_c_kernel(q_ref, k_ref, v_ref, tc_ref, o_ref, *, scale, rows):
    b = pl.program_id(2)
    nb = rows // NA_QROWS
    kblk = NA_KROWS // NA_QROWS
    kb = jnp.clip(b - 1, 0, nb - kblk)
    start = pl.multiple_of(kb * NA_TILE, NA_TILE)
    nk = NA_KROWS * GRID_W
    kw = k_ref[pl.ds(start, nk), :]
    vw = v_ref[pl.ds(start, nk), :]
    s = lax.dot_general(q_ref[...], kw, (((1,), (1,)), ((), ())), preferred_element_type=F32) * scale
    lane = lax.broadcasted_iota(jnp.int32, (GRID_W, 2 * GRID_W), 1)
    strips = []
    for ql in range(NA_QROWS):
        i = b * NA_QROWS + ql
        rstart = jnp.clip(i - NA_ROWS // 2, 0, rows - NA_ROWS)
        pieces = []
        for pr in range(NA_KROWS // 2):
            kr = kb * NA_QROWS + 2 * pr
            a = kr - i + (NA_ROWS - 1)
            v_lo = ((kr >= rstart) & (kr < rstart + NA_ROWS)).astype(jnp.int32)
            v_hi = ((kr + 1 >= rstart) & (kr + 1 < rstart + NA_ROWS)).astype(jnp.int32)
            valid = jnp.where(lane < GRID_W, v_lo, v_hi) > 0
            tb = tc_ref[jnp.clip(a + 1, 0, 2 * NA_ROWS - 1)]
            pieces.append(jnp.where(valid, tb, NEG_INF))
        strips.append(jnp.concatenate(pieces, axis=1))
    s = s + jnp.concatenate(strips, axis=0)
    m = jnp.max(s, axis=-1, keepdims=True)
    p = jnp.exp(s - m)
    l = jnp.sum(p, axis=-1, keepdims=True)
    o = jnp.dot(p.astype(vw.dtype), vw, preferred_element_type=F32) / l
    o_ref[...] = o.astype(o_ref.dtype)


def _attn_c(proj3, tc):
    b, s, _ = proj3.shape
    rows = s // GRID_W
    assert s % GRID_W == 0 and rows % NA_QROWS == 0 and rows >= NA_KROWS
    kv_spec = lambda off: pl.BlockSpec((None, s, HEAD_DIM), lambda bi, h, qi: (bi, 0, off // HEAD_DIM + h))
    return pl.pallas_call(
        functools.partial(_attn_c_kernel, scale=HEAD_DIM ** -0.5, rows=rows),
        grid=(b, H_C, s // NA_TILE),
        in_specs=[pl.BlockSpec((None, NA_TILE, HEAD_DIM), lambda bi, h, qi: (bi, qi, OFF_QC // HEAD_DIM + h)),
                  kv_spec(OFF_KC), kv_spec(OFF_VC),
                  pl.BlockSpec((None, 2 * NA_ROWS, GRID_W, 2 * GRID_W), lambda bi, h, qi: (h, 0, 0, 0))],
        out_specs=pl.BlockSpec((None, NA_TILE, HEAD_DIM), lambda bi, h, qi: (bi, qi, h)),
        out_shape=jax.ShapeDtypeStruct((b, s, W_C), BF16),
        compiler_params=_params(("parallel", "parallel", "arbitrary")),
        name="attn_neighbourhood",
    )(proj3, proj3, proj3, tc)


def _layer(x, xb, b, s, lw, tabs, band_tbl):
    m, d = x.shape
    proj = _in_proj(xb, lw["w_in"], tabs, lw["g_qn"], lw["g_kn"], s)
    proj3 = proj.reshape(b, s, IN_WIDTH)
    oa = _attn_a(proj3, band_tbl).reshape(m, W_A)
    ob = _attn_b(proj3).reshape(m, W_BQ)
    oc = _attn_c(proj3, lw["tc"]).reshape(m, W_C)
    mixed = _mix(oa, ob, oc, lw["g_grp"])
    y = _mm_resid(mixed, lw["w_out"], x, tm=1024, tn=512)
    x, xb = _layer_norm(y, lw["ln1_g"], lw["ln1_b"])
    hdn = _gate_up(xb, lw["w_gate"], lw["w_up"], tm=1024, tn=256)
    y = _mm_resid(hdn, lw["w_down"], x, tm=512, tn=256)
    return _layer_norm(y, lw["ln2_g"], lw["ln2_b"])


def kernel(x_prompt, x_sample, w_in, g_qn, g_kn, rpb, g_grp, w_out, ln1_g, ln1_b, w_gate, w_up, w_down,
           ln2_g, ln2_b):
    depth = w_in.shape[0]
    layers = []
    for l in range(depth):
        layers.append(dict(
            w_in=w_in[l].astype(BF16), w_out=w_out[l].astype(BF16), w_gate=w_gate[l].astype(BF16),
            w_up=w_up[l].astype(BF16), w_down=w_down[l].astype(BF16),
            g_qn=g_qn[l].reshape(1, HEAD_DIM), g_kn=g_kn[l].reshape(1, HEAD_DIM),
            tc=_rpb_table(rpb[l]), g_grp=g_grp[l],
            ln1_g=ln1_g[l], ln1_b=ln1_b[l], ln2_g=ln2_g[l], ln2_b=ln2_b[l]))
    band_tbl = jnp.asarray(_band_bias_table())

    def trunk(x3):
        b, s, d = x3.shape
        tabs = _rope_tables(s)
        x = x3.reshape(b * s, d)
        xb = x.astype(BF16)
        for lw in layers:
            x, xb = _layer(x, xb, b, s, lw, tabs, band_tbl)
        return x.reshape(b, s, d)

    return (trunk(x_prompt), trunk(x_sample))
```

```python
import functools

import numpy as np
import jax
import jax.numpy as jnp
from jax import lax
from jax.experimental import pallas as pl
from jax.experimental.pallas import tpu as pltpu

F32 = jnp.float32
BF16 = jnp.bfloat16

HEAD_DIM = 128
H_A, H_B, HKV_B, H_C = 8, 16, 4, 8
GRP_B = H_B // HKV_B
DILATED_PATTERNS = ((128, 1), (512, 4), (2048, 16))
ROPE_THETA = 500000.0
ROPE_DIM = HEAD_DIM // 4
AXIAL_THETA = 10000.0
GRID_W = 64
NA_ROWS = 8
NA_COLS = 16
LN_EPS = 1e-5
RMS_EPS = 1e-6
NEG_INF = -1e30
LOG2_E = 1.4426950408889634
Q_SCALE = HEAD_DIM ** -0.5 * LOG2_E

W_A = H_A * HEAD_DIM
W_BQ = H_B * HEAD_DIM
W_BKV = HKV_B * HEAD_DIM
W_C = H_C * HEAD_DIM
OFF_QA, OFF_KA, OFF_VA = 0, W_A, 2 * W_A
OFF_QB = 3 * W_A
OFF_KB = OFF_QB + W_BQ
OFF_VB = OFF_KB + W_BKV
OFF_QC = OFF_VB + W_BKV
OFF_KC = OFF_QC + W_C
OFF_VC = OFF_KC + W_C
IN_WIDTH = OFF_VC + W_C

LANES = 128
VMEM_LIMIT_BYTES = 52 * 1024 * 1024

BAND_REACH = max(w // 2 for w, _ in DILATED_PATTERNS)
BAND_TILE = 256
NA_QROWS = 4
NA_TILE = NA_QROWS * GRID_W
NA_KROWS = 12


def _params(sem):
    return pltpu.CompilerParams(dimension_semantics=sem, vmem_limit_bytes=VMEM_LIMIT_BYTES)


def _rot(h, c, s_up, s_dn, shift):
    return h * c + pltpu.roll(h, LANES - shift, 1) * s_up + pltpu.roll(h, shift, 1) * s_dn


def _inproj_kernel(x_ref, w_ref, ca_ref, ua_ref, da_ref, cb_ref, ub_ref, db_ref, gq_ref, gk_ref,
                   o_ref, acc_ref, *, tn):
    j = pl.program_id(1)
    acc_ref[...] = jnp.dot(x_ref[...], w_ref[...], preferred_element_type=F32)
    heads = tn // HEAD_DIM
    is_rope_a = j < OFF_VA // tn
    is_qb = (j >= OFF_QB // tn) & (j < OFF_KB // tn)
    is_kb = (j >= OFF_KB // tn) & (j < OFF_VB // tn)
    is_plain = jnp.logical_not(is_rope_a | is_qb | is_kb)
    is_q = (j < OFF_KA // tn) | is_qb | ((j >= OFF_QC // tn) & (j < OFF_KC // tn))
    qs = jnp.where(is_q, Q_SCALE, 1.0).astype(F32)

    @pl.when(is_plain)
    def _():
        o_ref[...] = (acc_ref[...] * qs).astype(o_ref.dtype)

    @pl.when(is_rope_a)
    def _():
        for hh in range(heads):
            sl = slice(hh * HEAD_DIM, (hh + 1) * HEAD_DIM)
            o_ref[:, sl] = (_rot(acc_ref[:, sl], ca_ref[...], ua_ref[...], da_ref[...],
                                 ROPE_DIM // 2) * qs).astype(o_ref.dtype)

    def norm_rope_b(g_ref):
        for hh in range(heads):
            sl = slice(hh * HEAD_DIM, (hh + 1) * HEAD_DIM)
            h = acc_ref[:, sl]
            h = h * lax.rsqrt(jnp.mean(h * h, axis=-1, keepdims=True) + RMS_EPS) * g_ref[...]
            o_ref[:, sl] = (_rot(h, cb_ref[...], ub_ref[...], db_ref[...],
                                 HEAD_DIM // 4) * qs).astype(o_ref.dtype)

    @pl.when(is_qb)
    def _():
        norm_rope_b(gq_ref)

    @pl.when(is_kb)
    def _():
        norm_rope_b(gk_ref)


def _in_proj(xb, w, tabs, gq, gk, seq):
    m, k = xb.shape
    n = w.shape[1]
    tm, tn = 1024, 512
    assert m % tm == 0 and n % tn == 0 and seq % tm == 0
    for off in (OFF_KA, OFF_VA, OFF_QB, OFF_KB, OFF_VB, OFF_QC, OFF_KC):
        assert off % tn == 0
    nseq = seq // tm
    tab_spec = pl.BlockSpec((tm, LANES), lambda i, j: (i % nseq, 0))
    g_spec = pl.BlockSpec((1, LANES), lambda i, j: (0, 0))
    return pl.pallas_call(
        functools.partial(_inproj_kernel, tn=tn),
        grid=(m // tm, n // tn),
        in_specs=[pl.BlockSpec((tm, k), lambda i, j: (i, 0)),
                  pl.BlockSpec((k, tn), lambda i, j: (0, j))] + [tab_spec] * 6 + [g_spec, g_spec],
        out_specs=pl.BlockSpec((tm, tn), lambda i, j: (i, j)),
        out_shape=jax.ShapeDtypeStruct((m, n), BF16),
        scratch_shapes=[pltpu.VMEM((tm, tn), F32)],
        compiler_params=_params(("parallel", "arbitrary")),
        name="in_proj",
    )(xb, w, *tabs, gq, gk)


def _rope_tables(seq):
    def cs(pos, theta, d):
        half = d // 2
        inv = theta ** (-jnp.arange(half, dtype=F32) * 2.0 / d)
        ang = pos.astype(F32)[:, None] * inv[None, :]
        return jnp.cos(ang), jnp.sin(ang)

    t = jnp.arange(seq)
    ca, sa = cs(t, ROPE_THETA, ROPE_DIM)
    pad = HEAD_DIM - ROPE_DIM
    z = lambda w_: jnp.zeros((seq, w_), F32)
    tab_ca = jnp.concatenate([ca, ca, jnp.ones((seq, pad), F32)], 1)
    tab_ua = jnp.concatenate([-sa, z(ROPE_DIM // 2 + pad)], 1)
    tab_da = jnp.concatenate([z(ROPE_DIM // 2), sa, z(pad)], 1)
    cr, sr = cs(t // GRID_W, AXIAL_THETA, HEAD_DIM // 2)
    cc, sc = cs(t % GRID_W, AXIAL_THETA, HEAD_DIM // 2)
    q = HEAD_DIM // 4
    tab_cb = jnp.concatenate([cr, cr, cc, cc], 1)
    tab_ub = jnp.concatenate([-sr, z(q), -sc, z(q)], 1)
    tab_db = jnp.concatenate([z(q), sr, z(q), sc], 1)
    return (tab_ca, tab_ua, tab_da, tab_cb, tab_ub, tab_db)


def _mm_resid_kernel(x_ref, w_ref, r_ref, o_ref, *, alpha):
    o_ref[...] = alpha * r_ref[...] + jnp.dot(x_ref[...], w_ref[...], preferred_element_type=F32)


def _mm_resid(xb, w, resid, alpha, tm, tn):
    m, k = xb.shape
    n = w.shape[1]
    assert m % tm == 0 and n % tn == 0
    return pl.pallas_call(
        functools.partial(_mm_resid_kernel, alpha=alpha),
        grid=(m // tm, n // tn),
        in_specs=[pl.BlockSpec((tm, k), lambda i, j: (i, 0)),
                  pl.BlockSpec((k, tn), lambda i, j: (0, j)),
                  pl.BlockSpec((tm, tn), lambda i, j: (i, j))],
        out_specs=pl.BlockSpec((tm, tn), lambda i, j: (i, j)),
        out_shape=jax.ShapeDtypeStruct((m, n), F32),
        compiler_params=_params(("parallel", "arbitrary")),
        name="mm_resid",
    )(xb, w, resid)


def _gateup_kernel(x_ref, wg_ref, wu_ref, o_ref):
    x = x_ref[...]
    g = jnp.dot(x, wg_ref[...], preferred_element_type=F32)
    u = jnp.dot(x, wu_ref[...], preferred_element_type=F32)
    o_ref[...] = (g / (1.0 + jnp.exp(-g)) * u).astype(o_ref.dtype)


def _gate_up(xb, wg, wu, tm, tn):
    m, k = xb.shape
    n = wg.shape[1]
    assert m % tm == 0 and n % tn == 0
    return pl.pallas_call(
        _gateup_kernel,
        grid=(m // tm, n // tn),
        in_specs=[pl.BlockSpec((tm, k), lambda i, j: (i, 0)),
                  pl.BlockSpec((k, tn), lambda i, j: (0, j)),
                  pl.BlockSpec((k, tn), lambda i, j: (0, j))],
        out_specs=pl.BlockSpec((tm, tn), lambda i, j: (i, j)),
        out_shape=jax.ShapeDtypeStruct((m, n), BF16),
        compiler_params=_params(("parallel", "arbitrary")),
        name="gate_up",
    )(xb, wg, wu)


def _ln_kernel(y_ref, g_ref, b_ref, o_ref, ob_ref):
    y = y_ref[...]
    mu = jnp.mean(y, axis=-1, keepdims=True)
    d = y - mu
    var = jnp.mean(d * d, axis=-1, keepdims=True)
    o = d * lax.rsqrt(var + LN_EPS) * g_ref[...] + b_ref[...]
    o_ref[...] = o
    ob_ref[...] = o.astype(ob_ref.dtype)


def _layer_norm(y, g, b, tm=256):
    m, d = y.shape
    row = pl.BlockSpec((tm, d), lambda i: (i, 0))
    vec = pl.BlockSpec((1, d), lambda i: (0, 0))
    return pl.pallas_call(
        _ln_kernel,
        grid=(m // tm,),
        in_specs=[row, vec, vec],
        out_specs=[row, row],
        out_shape=[jax.ShapeDtypeStruct((m, d), F32), jax.ShapeDtypeStruct((m, d), BF16)],
        compiler_params=_params(("parallel",)),
        name="layer_norm",
    )(y, g.reshape(1, d), b.reshape(1, d))


def _mix_kernel(oa_ref, ob_ref, oc_ref, g_ref, o_ref):
    off = 0
    for ref in (oa_ref, ob_ref, oc_ref):
        w = ref.shape[1]
        x = ref[...].astype(F32)
        xn = x * lax.rsqrt(jnp.mean(x * x, axis=-1, keepdims=True) + RMS_EPS)
        o_ref[:, off:off + w] = (xn * g_ref[:, off:off + w]).astype(o_ref.dtype)
        off += w


def _mix(oa, ob, oc, g, tm=512):
    m = oa.shape[0]
    d = oa.shape[1] + ob.shape[1] + oc.shape[1]
    return pl.pallas_call(
        _mix_kernel,
        grid=(m // tm,),
        in_specs=[pl.BlockSpec((tm, oa.shape[1]), lambda i: (i, 0)),
                  pl.BlockSpec((tm, ob.shape[1]), lambda i: (i, 0)),
                  pl.BlockSpec((tm, oc.shape[1]), lambda i: (i, 0)),
                  pl.BlockSpec((1, d), lambda i: (0, 0))],
        out_specs=pl.BlockSpec((tm, d), lambda i: (i, 0)),
        out_shape=jax.ShapeDtypeStruct((m, d), BF16),
        compiler_params=_params(("parallel",)),
        name="group_rms_mix",
    )(oa, ob, oc, g.reshape(1, d))


def _qk(q, k):
    return lax.dot_general(q, k, (((1,), (1,)), ((), ())), preferred_element_type=F32)


def _exp2_rows(s, m):
    return jnp.concatenate([jnp.exp2(s[:, c * LANES:(c + 1) * LANES] - m) for c in range(s.shape[1] // LANES)],
                           axis=1)


def _attn_b_kernel(q_ref, k_ref, v_ref, o_ref, m_ref, l_ref, acc_ref, vx_ref, *, chain):
    kv = pl.program_id(3)
    tq = q_ref.shape[0]
    tk = k_ref.shape[0]

    @pl.when(kv == 0)
    def _():
        m_ref[...] = jnp.full(m_ref.shape, -jnp.inf, F32)
        l_ref[...] = jnp.zeros(l_ref.shape, F32)
        acc_ref[...] = jnp.zeros(acc_ref.shape, F32)
        vx_ref[:, HEAD_DIM:] = jnp.ones((tk, HEAD_DIM), vx_ref.dtype)

    vx_ref[:, :HEAD_DIM] = v_ref[...]
    k = k_ref[...]
    vx = vx_ref[...]
    for r in range(GRP_B):
        for c in range(tq // chain):
            rows = slice(r * tq + c * chain, r * tq + (c + 1) * chain)
            s = _qk(q_ref[c * chain:(c + 1) * chain, r * HEAD_DIM:(r + 1) * HEAD_DIM], k)
            m_prev = m_ref[rows]
            m_new = jnp.maximum(m_prev, jnp.max(s, axis=-1, keepdims=True))
            alpha = jnp.exp2(m_prev - m_new)
            p = _exp2_rows(s, m_new)
            pv = jnp.dot(p.astype(vx.dtype), vx, preferred_element_type=F32)
            acc_ref[rows] = alpha * acc_ref[rows] + pv[:, :HEAD_DIM]
            l_ref[rows] = alpha * l_ref[rows] + pv[:, HEAD_DIM:]
            m_ref[rows] = m_new

    @pl.when(kv == pl.num_programs(3) - 1)
    def _():
        o = acc_ref[...] / l_ref[...]
        for r in range(GRP_B):
            o_ref[:, r * HEAD_DIM:(r + 1) * HEAD_DIM] = o[r * tq:(r + 1) * tq].astype(o_ref.dtype)


def _attn_b(proj3, tq=512, tk=2048, chain=256):
    b, s, _ = proj3.shape
    qw = GRP_B * HEAD_DIM
    assert s % tq == 0 and s % tk == 0 and tq % chain == 0 and OFF_QB % qw == 0
    stat = pltpu.VMEM((GRP_B * tq, HEAD_DIM), F32)
    return pl.pallas_call(
        functools.partial(_attn_b_kernel, chain=chain),
        grid=(b, HKV_B, s // tq, s // tk),
        in_specs=[pl.BlockSpec((None, tq, qw), lambda bi, g, qi, ki: (bi, qi, OFF_QB // qw + g)),
                  pl.BlockSpec((None, tk, HEAD_DIM), lambda bi, g, qi, ki: (bi, ki, OFF_KB // HEAD_DIM + g)),
                  pl.BlockSpec((None, tk, HEAD_DIM), lambda bi, g, qi, ki: (bi, ki, OFF_VB // HEAD_DIM + g))],
        out_specs=pl.BlockSpec((None, tq, qw), lambda bi, g, qi, ki: (bi, qi, g)),
        out_shape=jax.ShapeDtypeStruct((b, s, W_BQ), BF16),
        scratch_shapes=[stat, stat, stat, pltpu.VMEM((tk, 2 * HEAD_DIM), BF16)],
        compiler_params=_params(("parallel", "parallel", "parallel", "arbitrary")),
        name="attn_axial_gqa",
    )(proj3, proj3, proj3)


def _band_bias_table():
    nt = 2 * BAND_REACH // BAND_TILE + 1
    u = np.arange(nt + 1)[:, None, None] - BAND_REACH // BAND_TILE
    rel = u * BAND_TILE + np.arange(BAND_TILE)[None, None, :] - np.arange(BAND_TILE)[None, :, None]
    mult = np.zeros(rel.shape, np.int64)
    for window, dil in DILATED_PATTERNS:
        mult += (np.abs(rel) <= window // 2) & (rel % dil == 0)
    mult[nt] = 0
    return np.where(mult > 0, np.log2(np.maximum(mult, 1)), NEG_INF).astype(np.float32)


def _attn_a_kernel(q_ref, k_ref, v_ref, tbl_ref, o_ref, vx_ref, *, nblk, wblk, heads):
    t = BAND_TILE
    r = BAND_REACH // t
    nt = 2 * r + 1
    qi = pl.program_id(2)
    sb = jnp.clip(qi - r, 0, nblk - wblk)

    @pl.when(qi == 0)
    def _():
        vx_ref[:, :, HEAD_DIM:] = jnp.ones((heads, wblk * t, HEAD_DIM), vx_ref.dtype)

    idx = []
    for c in range(wblk):
        u = sb + c - qi + r
        idx.append(jnp.where((u >= 0) & (u < nt), u, nt))
    for hh in range(heads):
        hs = slice(hh * HEAD_DIM, (hh + 1) * HEAD_DIM)
        q = q_ref[:, hs]
        s = []
        for c in range(wblk):
            rows = pl.ds(pl.multiple_of((sb + c) * t, t), t)
            vx_ref[hh, c * t:(c + 1) * t, :HEAD_DIM] = v_ref[rows, hs]
            s.append(_qk(q, k_ref[rows, hs]) + tbl_ref[idx[c]])
        m = functools.reduce(jnp.maximum, s)
        m = functools.reduce(jnp.maximum, [m[:, c * LANES:(c + 1) * LANES] for c in range(t // LANES)])
        m = jnp.broadcast_to(jnp.max(m, axis=-1, keepdims=True), (t, LANES))
        pv = jnp.zeros((t, 2 * HEAD_DIM), F32)
        for c in range(wblk):
            p = _exp2_rows(s[c], m).astype(vx_ref.dtype)
            pv = pv + jnp.dot(p, vx_ref[hh, c * t:(c + 1) * t, :], preferred_element_type=F32)
        o_ref[:, hs] = (pv[:, :HEAD_DIM] / pv[:, HEAD_DIM:]).astype(o_ref.dtype)


def _attn_a(proj3, band_tbl, heads=2):
    b, s, _ = proj3.shape
    t = BAND_TILE
    nblk = s // t
    wblk = min(nblk, 2 * BAND_REACH // t + 1)
    hw = heads * HEAD_DIM
    assert s % t == 0 and H_A % heads == 0 and OFF_KA % hw == 0 and OFF_VA % hw == 0
    kv_spec = lambda off: pl.BlockSpec((None, s, hw), lambda bi, h, qi: (bi, 0, off // hw + h),
                                       pipeline_mode=pl.Buffered(1))
    return pl.pallas_call(
        functools.partial(_attn_a_kernel, nblk=nblk, wblk=wblk, heads=heads),
        grid=(b, H_A // heads, nblk),
        in_specs=[pl.BlockSpec((None, t, hw), lambda bi, h, qi: (bi, qi, OFF_QA // hw + h)),
                  kv_spec(OFF_KA), kv_spec(OFF_VA),
                  pl.BlockSpec(band_tbl.shape, lambda bi, h, qi: (0, 0, 0), pipeline_mode=pl.Buffered(1))],
        out_specs=pl.BlockSpec((None, t, hw), lambda bi, h, qi: (bi, qi, h)),
        out_shape=jax.ShapeDtypeStruct((b, s, W_A), BF16),
        scratch_shapes=[pltpu.VMEM((heads, wblk * t, 2 * HEAD_DIM), BF16)],
        compiler_params=_params(("parallel", "parallel", "arbitrary")),
        name="attn_dilated",
    )(proj3, proj3, proj3, band_tbl)


def _rpb_table_kernel(rpb_ref, o_ref):
    h = pl.program_id(0)
    a = pl.program_id(1) - 1
    n_roff, n_coff = 2 * NA_ROWS - 1, 2 * NA_COLS - 1
    shape = (GRID_W, 2 * GRID_W)
    qc = lax.broadcasted_iota(jnp.int32, shape, 0)
    lane = lax.broadcasted_iota(jnp.int32, shape, 1)
    hi = lane >= GRID_W
    kc = jnp.where(hi, lane - GRID_W, lane)
    coff = jnp.clip(kc - qc, -(NA_COLS - 1), NA_COLS - 1) + (NA_COLS - 1)
    cstart = jnp.clip(qc - NA_COLS // 2, 0, GRID_W - NA_COLS)
    cmask = (kc >= cstart) & (kc < cstart + NA_COLS)
    a_lo = jnp.clip(a, 0, n_roff - 1)
    a_hi = jnp.clip(a + 1, 0, n_roff - 1)
    val = jnp.zeros(shape, F32)
    for bb in range(n_coff):
        r_lo = rpb_ref[(h * n_roff + a_lo) * n_coff + bb]
        r_hi = rpb_ref[(h * n_roff + a_hi) * n_coff + bb]
        val = jnp.where(coff == bb, jnp.where(hi, r_hi, r_lo), val)
    ok_lo = ((a >= 0) & (a < n_roff)).astype(jnp.int32)
    ok_hi = ((a + 1 >= 0) & (a + 1 < n_roff)).astype(jnp.int32)
    ok = jnp.where(hi, ok_hi, ok_lo) > 0
    o_ref[0, 0] = jnp.where(cmask & ok, val * LOG2_E, NEG_INF)


def _rpb_table(rpb):
    h = rpb.shape[0]
    n_slots = 2 * NA_ROWS
    return pl.pallas_call(
        _rpb_table_kernel,
        grid=(h, n_slots),
        in_specs=[pl.BlockSpec(memory_space=pltpu.SMEM)],
        out_specs=pl.BlockSpec((1, 1, GRID_W, 2 * GRID_W), lambda hi, ai: (hi, ai, 0, 0)),
        out_shape=jax.ShapeDtypeStruct((h, n_slots, GRID_W, 2 * GRID_W), F32),
        compiler_params=_params(("arbitrary", "arbitrary")),
        name="rpb_table",
    )(rpb.reshape(-1))


def _attn_c_kernel(q_ref, k_ref, v_ref, tc_ref, o_ref, vx_ref, *, rows, heads):
    b = pl.program_id(2)
    nb = rows // NA_QROWS
    kblk = NA_KROWS // NA_QROWS
    kb = jnp.clip(b - 1, 0, nb - kblk)
    nk = NA_KROWS * GRID_W
    win = pl.ds(pl.multiple_of(kb * NA_TILE, NA_TILE), nk)

    @pl.when(b == 0)
    def _():
        vx_ref[:, :, HEAD_DIM:] = jnp.ones((heads, nk, HEAD_DIM), vx_ref.dtype)

    lane = lax.broadcasted_iota(jnp.int32, (GRID_W, 2 * GRID_W), 1)
    for hh in range(heads):
        hs = slice(hh * HEAD_DIM, (hh + 1) * HEAD_DIM)
        vx_ref[hh, :, :HEAD_DIM] = v_ref[win, hs]
        strips = []
        for ql in range(NA_QROWS):
            i = b * NA_QROWS + ql
            rstart = jnp.clip(i - NA_ROWS // 2, 0, rows - NA_ROWS)
            pieces = []
            for pr in range(NA_KROWS // 2):
                kr = kb * NA_QROWS + 2 * pr
                a = kr - i + (NA_ROWS - 1)
                v_lo = ((kr >= rstart) & (kr < rstart + NA_ROWS)).astype(jnp.int32)
                v_hi = ((kr + 1 >= rstart) & (kr + 1 < rstart + NA_ROWS)).astype(jnp.int32)
                valid = jnp.where(lane < GRID_W, v_lo, v_hi) > 0
                tb = tc_ref[hh, jnp.clip(a + 1, 0, 2 * NA_ROWS - 1)]
                pieces.append(jnp.where(valid, tb, NEG_INF))
            strips.append(jnp.concatenate(pieces, axis=1))
        s = _qk(q_ref[:, hs], k_ref[win, hs]) + jnp.concatenate(strips, axis=0)
        m = functools.reduce(jnp.maximum, [s[:, c * LANES:(c + 1) * LANES] for c in range(nk // LANES)])
        m = jnp.broadcast_to(jnp.max(m, axis=-1, keepdims=True), (NA_TILE, LANES))
        p = _exp2_rows(s, m).astype(vx_ref.dtype)
        pv = jnp.dot(p, vx_ref[hh], preferred_element_type=F32)
        o_ref[:, hs] = (pv[:, :HEAD_DIM] / pv[:, HEAD_DIM:]).astype(o_ref.dtype)


def _attn_c(proj3, tc, heads=2):
    b, s, _ = proj3.shape
    rows = s // GRID_W
    hw = heads * HEAD_DIM
    assert s % GRID_W == 0 and rows % NA_QROWS == 0 and rows >= NA_KROWS
    assert H_C % heads == 0 and OFF_QC % hw == 0 and OFF_KC % hw == 0 and OFF_VC % hw == 0
    kv_spec = lambda off: pl.BlockSpec((None, s, hw), lambda bi, h, qi: (bi, 0, off // hw + h),
                                       pipeline_mode=pl.Buffered(1))
    return pl.pallas_call(
        functools.partial(_attn_c_kernel, rows=rows, heads=heads),
        grid=(b, H_C // heads, s // NA_TILE),
        in_specs=[pl.BlockSpec((None, NA_TILE, hw), lambda bi, h, qi: (bi, qi, OFF_QC // hw + h)),
                  kv_spec(OFF_KC), kv_spec(OFF_VC),
                  pl.BlockSpec((heads, 2 * NA_ROWS, GRID_W, 2 * GRID_W), lambda bi, h, qi: (h, 0, 0, 0))],
        out_specs=pl.BlockSpec((None, NA_TILE, hw), lambda bi, h, qi: (bi, qi, h)),
        out_shape=jax.ShapeDtypeStruct((b, s, W_C), BF16),
        scratch_shapes=[pltpu.VMEM((heads, NA_KROWS * GRID_W, 2 * HEAD_DIM), BF16)],
        compiler_params=_params(("parallel", "parallel", "arbitrary")),
        name="attn_neighbourhood",
    )(proj3, proj3, proj3, tc)


def _layer(x, xb, b, s, lw, tabs, band_tbl):
    m, d = x.shape
    proj = _in_proj(xb, lw["w_in"], tabs, lw["g_qn"], lw["g_kn"], s)
    proj3 = proj.reshape(b, s, IN_WIDTH)
    oa = _attn_a(proj3, band_tbl).reshape(m, W_A)
    ob = _attn_b(proj3).reshape(m, W_BQ)
    oc = _attn_c(proj3, lw["tc"]).reshape(m, W_C)
    mixed = _mix(oa, ob, oc, lw["g_grp"])
    y = _mm_resid(mixed, lw["w_out"], x, lw["alpha"], tm=1024, tn=512)
    x, xb = _layer_norm(y, lw["ln1_g"], lw["ln1_b"])
    hdn = _gate_up(xb, lw["w_gate"], lw["w_up"], tm=1024, tn=256)
    y = _mm_resid(hdn, lw["w_down"], x, lw["alpha"], tm=512, tn=256)
    return _layer_norm(y, lw["ln2_g"], lw["ln2_b"])


def kernel(x_prompt, x_sample, w_in, g_qn, g_kn, rpb, g_grp, w_out, ln1_g, ln1_b, w_gate, w_up, w_down,
           ln2_g, ln2_b):
    depth = w_in.shape[0]
    layers = []
    for l in range(depth):
        layers.append(dict(
            w_in=w_in[l].astype(BF16), w_out=w_out[l].astype(BF16), w_gate=w_gate[l].astype(BF16),
            w_up=w_up[l].astype(BF16), w_down=w_down[l].astype(BF16),
            g_qn=g_qn[l].reshape(1, HEAD_DIM), g_kn=g_kn[l].reshape(1, HEAD_DIM),
            tc=_rpb_table(rpb[l]), g_grp=g_grp[l], alpha=(2.0 * depth) ** 0.25,
            ln1_g=ln1_g[l], ln1_b=ln1_b[l], ln2_g=ln2_g[l], ln2_b=ln2_b[l]))
    band_tbl = jnp.asarray(_band_bias_table())

    def trunk(x3):
        b, s, d = x3.shape
        tabs = _rope_tables(s)
        x = x3.reshape(b * s, d)
        xb = x.astype(BF16)
        for lw in layers:
            x, xb = _layer(x, xb, b, s, lw, tabs, band_tbl)
        return x.reshape(b, s, d)

    return (trunk(x_prompt), trunk(x_sample))
```

```python
import functools

import numpy as np
import jax
import jax.numpy as jnp
from jax import lax
from jax.experimental import pallas as pl
from jax.experimental.pallas import tpu as pltpu

F32 = jnp.float32
BF16 = jnp.bfloat16

HEAD_DIM = 128
H_A, H_B, HKV_B, H_C = 8, 16, 4, 8
GRP_B = H_B // HKV_B
DILATED_PATTERNS = ((128, 1), (512, 4), (2048, 16))
ROPE_THETA = 500000.0
ROPE_DIM = HEAD_DIM // 4
AXIAL_THETA = 10000.0
GRID_W = 64
NA_ROWS = 8
NA_COLS = 16
LN_EPS = 1e-5
RMS_EPS = 1e-6
NEG_INF = -1e30
LOG2_E = 1.4426950408889634
Q_SCALE = HEAD_DIM ** -0.5 * LOG2_E

W_A = H_A * HEAD_DIM
W_BQ = H_B * HEAD_DIM
W_BKV = HKV_B * HEAD_DIM
W_C = H_C * HEAD_DIM
OFF_QA, OFF_KA, OFF_VA = 0, W_A, 2 * W_A
OFF_QB = 3 * W_A
OFF_KB = OFF_QB + W_BQ
OFF_VB = OFF_KB + W_BKV
OFF_QC = OFF_VB + W_BKV
OFF_KC = OFF_QC + W_C
OFF_VC = OFF_KC + W_C
IN_WIDTH = OFF_VC + W_C

LANES = 128
VMEM_LIMIT_BYTES = 52 * 1024 * 1024

BAND_REACH = max(w // 2 for w, _ in DILATED_PATTERNS)
BAND_TILE = 256
NA_QROWS = 4
NA_TILE = NA_QROWS * GRID_W
NA_KROWS = 12


def _params(sem):
    return pltpu.CompilerParams(dimension_semantics=sem, vmem_limit_bytes=VMEM_LIMIT_BYTES)


def _rot(h, c, s_up, s_dn, shift):
    return h * c + pltpu.roll(h, LANES - shift, 1) * s_up + pltpu.roll(h, shift, 1) * s_dn


def _inproj_kernel(x_ref, w_ref, ca_ref, ua_ref, da_ref, cb_ref, ub_ref, db_ref, gq_ref, gk_ref,
                   o_ref, *, tn, nchunk):
    j = pl.program_id(1)
    rc = x_ref.shape[0] // nchunk
    heads = tn // HEAD_DIM
    is_rope_a = j < OFF_VA // tn
    is_qb = (j >= OFF_QB // tn) & (j < OFF_KB // tn)
    is_kb = (j >= OFF_KB // tn) & (j < OFF_VB // tn)
    is_plain = jnp.logical_not(is_rope_a | is_qb | is_kb)
    is_q = (j < OFF_KA // tn) | is_qb | ((j >= OFF_QC // tn) & (j < OFF_KC // tn))
    qs = jnp.where(is_q, Q_SCALE, 1.0).astype(F32)

    def run(epilogue):
        for c in range(nchunk):
            rows = slice(c * rc, (c + 1) * rc)
            acc = jnp.dot(x_ref[rows, :], w_ref[...], preferred_element_type=F32)
            for hh in range(heads):
                sl = slice(hh * HEAD_DIM, (hh + 1) * HEAD_DIM)
                o_ref[rows, sl] = (epilogue(acc[:, sl], rows) * qs).astype(o_ref.dtype)

    def rope_a(h, rows):
        return _rot(h, ca_ref[rows, :], ua_ref[rows, :], da_ref[rows, :], ROPE_DIM // 2)

    def norm_rope_b(g_ref):
        def f(h, rows):
            h = h * lax.rsqrt(jnp.mean(h * h, axis=-1, keepdims=True) + RMS_EPS) * g_ref[...]
            return _rot(h, cb_ref[rows, :], ub_ref[rows, :], db_ref[rows, :], HEAD_DIM // 4)
        return f

    pl.when(is_plain)(lambda: run(lambda h, rows: h))
    pl.when(is_rope_a)(lambda: run(rope_a))
    pl.when(is_qb)(lambda: run(norm_rope_b(gq_ref)))
    pl.when(is_kb)(lambda: run(norm_rope_b(gk_ref)))


def _w_spec(w):
    _, k, tn = w.shape
    return pl.BlockSpec((None, k, tn), lambda i, j: (j, 0, 0))


def _in_proj(xb, w, tabs, gq, gk, seq, tm=1024, nchunk=4):
    m, k = xb.shape
    nt, _, tn = w.shape
    assert m % tm == 0 and seq % tm == 0 and tm % nchunk == 0
    for off in (OFF_KA, OFF_VA, OFF_QB, OFF_KB, OFF_VB, OFF_QC, OFF_KC):
        assert off % tn == 0
    nseq = seq // tm
    tab_spec = pl.BlockSpec((tm, LANES), lambda i, j: (i % nseq, 0))
    g_spec = pl.BlockSpec((1, LANES), lambda i, j: (0, 0))
    return pl.pallas_call(
        functools.partial(_inproj_kernel, tn=tn, nchunk=nchunk),
        grid=(m // tm, nt),
        in_specs=[pl.BlockSpec((tm, k), lambda i, j: (i, 0)), _w_spec(w)] + [tab_spec] * 6 + [g_spec, g_spec],
        out_specs=pl.BlockSpec((tm, tn), lambda i, j: (i, j)),
        out_shape=jax.ShapeDtypeStruct((m, nt * tn), BF16),
        compiler_params=_params(("parallel", "arbitrary")),
        name="in_proj",
    )(xb, w, *tabs, gq, gk)


def _cast_kernel(w_ref, o_ref):
    o_ref[...] = w_ref[...].astype(o_ref.dtype)


def _cast_tiled(w3, layer, tn):
    _, k, n = w3.shape
    assert n % tn == 0
    return pl.pallas_call(
        _cast_kernel,
        grid=(n // tn,),
        in_specs=[pl.BlockSpec((None, k, tn), lambda j: (layer, 0, j))],
        out_specs=pl.BlockSpec((None, k, tn), lambda j: (j, 0, 0)),
        out_shape=jax.ShapeDtypeStruct((n // tn, k, tn), BF16),
        compiler_params=_params(("parallel",)),
        name="cast_weight",
    )(w3)


def _rope_tables(seq):
    def cs(pos, theta, d):
        half = d // 2
        inv = theta ** (-jnp.arange(half, dtype=F32) * 2.0 / d)
        ang = pos.astype(F32)[:, None] * inv[None, :]
        return jnp.cos(ang), jnp.sin(ang)

    t = jnp.arange(seq)
    ca, sa = cs(t, ROPE_THETA, ROPE_DIM)
    pad = HEAD_DIM - ROPE_DIM
    z = lambda w_: jnp.zeros((seq, w_), F32)
    tab_ca = jnp.concatenate([ca, ca, jnp.ones((seq, pad), F32)], 1)
    tab_ua = jnp.concatenate([-sa, z(ROPE_DIM // 2 + pad)], 1)
    tab_da = jnp.concatenate([z(ROPE_DIM // 2), sa, z(pad)], 1)
    cr, sr = cs(t // GRID_W, AXIAL_THETA, HEAD_DIM // 2)
    cc, sc = cs(t % GRID_W, AXIAL_THETA, HEAD_DIM // 2)
    q = HEAD_DIM // 4
    tab_cb = jnp.concatenate([cr, cr, cc, cc], 1)
    tab_ub = jnp.concatenate([-sr, z(q), -sc, z(q)], 1)
    tab_db = jnp.concatenate([z(q), sr, z(q), sc], 1)
    return (tab_ca, tab_ua, tab_da, tab_cb, tab_ub, tab_db)


def _mm_resid_kernel(x_ref, w_ref, r_ref, o_ref, *, alpha):
    o_ref[...] = alpha * r_ref[...] + jnp.dot(x_ref[...], w_ref[...], preferred_element_type=F32)


def _mm_resid(xb, w, resid, alpha, tm):
    m, k = xb.shape
    nt, _, tn = w.shape
    assert m % tm == 0
    return pl.pallas_call(
        functools.partial(_mm_resid_kernel, alpha=alpha),
        grid=(m // tm, nt),
        in_specs=[pl.BlockSpec((tm, k), lambda i, j: (i, 0)), _w_spec(w),
                  pl.BlockSpec((tm, tn), lambda i, j: (i, j))],
        out_specs=pl.BlockSpec((tm, tn), lambda i, j: (i, j)),
        out_shape=jax.ShapeDtypeStruct((m, nt * tn), F32),
        compiler_params=_params(("parallel", "arbitrary")),
        name="mm_resid",
    )(xb, w, resid)


def _gateup_kernel(x_ref, wg_ref, wu_ref, o_ref):
    x = x_ref[...]
    g = jnp.dot(x, wg_ref[...], preferred_element_type=F32)
    u = jnp.dot(x, wu_ref[...], preferred_element_type=F32)
    o_ref[...] = (g / (1.0 + jnp.exp(-g)) * u).astype(o_ref.dtype)


def _gate_up(xb, wg, wu, tm):
    m, k = xb.shape
    nt, _, tn = wg.shape
    assert m % tm == 0 and wu.shape == wg.shape
    return pl.pallas_call(
        _gateup_kernel,
        grid=(m // tm, nt),
        in_specs=[pl.BlockSpec((tm, k), lambda i, j: (i, 0)), _w_spec(wg), _w_spec(wu)],
        out_specs=pl.BlockSpec((tm, tn), lambda i, j: (i, j)),
        out_shape=jax.ShapeDtypeStruct((m, nt * tn), BF16),
        compiler_params=_params(("parallel", "arbitrary")),
        name="gate_up",
    )(xb, wg, wu)


def _ln_kernel(y_ref, g_ref, b_ref, o_ref, ob_ref):
    y = y_ref[...]
    mu = jnp.mean(y, axis=-1, keepdims=True)
    d = y - mu
    var = jnp.mean(d * d, axis=-1, keepdims=True)
    o = d * lax.rsqrt(var + LN_EPS) * g_ref[...] + b_ref[...]
    o_ref[...] = o
    ob_ref[...] = o.astype(ob_ref.dtype)


def _layer_norm(y, g, b, tm=256):
    m, d = y.shape
    row = pl.BlockSpec((tm, d), lambda i: (i, 0))
    vec = pl.BlockSpec((1, d), lambda i: (0, 0))
    return pl.pallas_call(
        _ln_kernel,
        grid=(m // tm,),
        in_specs=[row, vec, vec],
        out_specs=[row, row],
        out_shape=[jax.ShapeDtypeStruct((m, d), F32), jax.ShapeDtypeStruct((m, d), BF16)],
        compiler_params=_params(("parallel",)),
        name="layer_norm",
    )(y, g.reshape(1, d), b.reshape(1, d))


def _mix_kernel(oa_ref, ob_ref, oc_ref, g_ref, o_ref):
    off = 0
    for ref in (oa_ref, ob_ref, oc_ref):
        w = ref.shape[1]
        x = ref[...].astype(F32)
        xn = x * lax.rsqrt(jnp.mean(x * x, axis=-1, keepdims=True) + RMS_EPS)
        o_ref[:, off:off + w] = (xn * g_ref[:, off:off + w]).astype(o_ref.dtype)
        off += w


def _mix(oa, ob, oc, g, tm=512):
    m = oa.shape[0]
    d = oa.shape[1] + ob.shape[1] + oc.shape[1]
    return pl.pallas_call(
        _mix_kernel,
        grid=(m // tm,),
        in_specs=[pl.BlockSpec((tm, oa.shape[1]), lambda i: (i, 0)),
                  pl.BlockSpec((tm, ob.shape[1]), lambda i: (i, 0)),
                  pl.BlockSpec((tm, oc.shape[1]), lambda i: (i, 0)),
                  pl.BlockSpec((1, d), lambda i: (0, 0))],
        out_specs=pl.BlockSpec((tm, d), lambda i: (i, 0)),
        out_shape=jax.ShapeDtypeStruct((m, d), BF16),
        compiler_params=_params(("parallel",)),
        name="group_rms_mix",
    )(oa, ob, oc, g.reshape(1, d))


def _qk(q, k):
    return lax.dot_general(q, k, (((1,), (1,)), ((), ())), preferred_element_type=F32)


def _exp2_rows(s, m):
    return jnp.concatenate([jnp.exp2(s[:, c * LANES:(c + 1) * LANES] - m) for c in range(s.shape[1] // LANES)],
                           axis=1)


def _attn_b_kernel(q_ref, k_ref, v_ref, o_ref, m_ref, l_ref, acc_ref, vx_ref, *, chain):
    kv = pl.program_id(3)
    tq = q_ref.shape[0]
    tk = k_ref.shape[0]

    @pl.when(kv == 0)
    def _():
        m_ref[...] = jnp.full(m_ref.shape, -jnp.inf, F32)
        l_ref[...] = jnp.zeros(l_ref.shape, F32)
        acc_ref[...] = jnp.zeros(acc_ref.shape, F32)
        vx_ref[:, HEAD_DIM:] = jnp.ones((tk, HEAD_DIM), vx_ref.dtype)

    vx_ref[:, :HEAD_DIM] = v_ref[...]
    k = k_ref[...]
    vx = vx_ref[...]
    for r in range(GRP_B):
        for c in range(tq // chain):
            rows = slice(r * tq + c * chain, r * tq + (c + 1) * chain)
            s = _qk(q_ref[c * chain:(c + 1) * chain, r * HEAD_DIM:(r + 1) * HEAD_DIM], k)
            m_prev = m_ref[rows]
            m_new = jnp.maximum(m_prev, jnp.max(s, axis=-1, keepdims=True))
            alpha = jnp.exp2(m_prev - m_new)
            p = _exp2_rows(s, m_new)
            pv = jnp.dot(p.astype(vx.dtype), vx, preferred_element_type=F32)
            acc_ref[rows] = alpha * acc_ref[rows] + pv[:, :HEAD_DIM]
            l_ref[rows] = alpha * l_ref[rows] + pv[:, HEAD_DIM:]
            m_ref[rows] = m_new

    @pl.when(kv == pl.num_programs(3) - 1)
    def _():
        o = acc_ref[...] / l_ref[...]
        for r in range(GRP_B):
            o_ref[:, r * HEAD_DIM:(r + 1) * HEAD_DIM] = o[r * tq:(r + 1) * tq].astype(o_ref.dtype)


def _attn_b(proj3, tq=2048, tk=2048, chain=256):
    b, s, _ = proj3.shape
    qw = GRP_B * HEAD_DIM
    assert s % tq == 0 and s % tk == 0 and tq % chain == 0 and OFF_QB % qw == 0
    stat = pltpu.VMEM((GRP_B * tq, HEAD_DIM), F32)
    return pl.pallas_call(
        functools.partial(_attn_b_kernel, chain=chain),
        grid=(b, HKV_B, s // tq, s // tk),
        in_specs=[pl.BlockSpec((None, tq, qw), lambda bi, g, qi, ki: (bi, qi, OFF_QB // qw + g)),
                  pl.BlockSpec((None, tk, HEAD_DIM), lambda bi, g, qi, ki: (bi, ki, OFF_KB // HEAD_DIM + g)),
                  pl.BlockSpec((None, tk, HEAD_DIM), lambda bi, g, qi, ki: (bi, ki, OFF_VB // HEAD_DIM + g))],
        out_specs=pl.BlockSpec((None, tq, qw), lambda bi, g, qi, ki: (bi, qi, g)),
        out_shape=jax.ShapeDtypeStruct((b, s, W_BQ), BF16),
        scratch_shapes=[stat, stat, stat, pltpu.VMEM((tk, 2 * HEAD_DIM), BF16)],
        compiler_params=_params(("parallel", "parallel", "parallel", "arbitrary")),
        name="attn_axial_gqa",
    )(proj3, proj3, proj3)


def _band_bias_table():
    nt = 2 * BAND_REACH // BAND_TILE + 1
    u = np.arange(nt + 1)[:, None, None] - BAND_REACH // BAND_TILE
    rel = u * BAND_TILE + np.arange(BAND_TILE)[None, None, :] - np.arange(BAND_TILE)[None, :, None]
    mult = np.zeros(rel.shape, np.int64)
    for window, dil in DILATED_PATTERNS:
        mult += (np.abs(rel) <= window // 2) & (rel % dil == 0)
    mult[nt] = 0
    return np.where(mult > 0, np.log2(np.maximum(mult, 1)), NEG_INF).astype(np.float32)


def _attn_a_kernel(q_ref, k_ref, v_ref, tbl_ref, o_ref, vx_ref, *, nblk, wblk, heads, qsub):
    t = BAND_TILE
    r = BAND_REACH // t
    nt = 2 * r + 1
    qi = pl.program_id(2)

    @pl.when(qi == 0)
    def _():
        vx_ref[:, :, HEAD_DIM:] = jnp.ones((qsub * heads, wblk * t, HEAD_DIM), vx_ref.dtype)

    for u in range(qsub):
        qt = qi * qsub + u
        sb = jnp.clip(qt - r, 0, nblk - wblk)
        idx = []
        for c in range(wblk):
            d = sb + c - qt + r
            idx.append(jnp.where((d >= 0) & (d < nt), d, nt))
        qrows = slice(u * t, (u + 1) * t)
        for hh in range(heads):
            ch = u * heads + hh
            hs = slice(hh * HEAD_DIM, (hh + 1) * HEAD_DIM)
            q = q_ref[qrows, hs]
            s = []
            for c in range(wblk):
                rows = pl.ds(pl.multiple_of((sb + c) * t, t), t)
                vx_ref[ch, c * t:(c + 1) * t, :HEAD_DIM] = v_ref[rows, hs]
                s.append(_qk(q, k_ref[rows, hs]) + tbl_ref[idx[c]])
            m = functools.reduce(jnp.maximum, s)
            m = functools.reduce(jnp.maximum, [m[:, c * LANES:(c + 1) * LANES] for c in range(t // LANES)])
            m = jnp.broadcast_to(jnp.max(m, axis=-1, keepdims=True), (t, LANES))
            pv = jnp.zeros((t, 2 * HEAD_DIM), F32)
            for c in range(wblk):
                p = _exp2_rows(s[c], m).astype(vx_ref.dtype)
                pv = pv + jnp.dot(p, vx_ref[ch, c * t:(c + 1) * t, :], preferred_element_type=F32)
            o_ref[qrows, hs] = (pv[:, :HEAD_DIM] / pv[:, HEAD_DIM:]).astype(o_ref.dtype)


def _attn_a(proj3, band_tbl, heads=2, qsub=4):
    b, s, _ = proj3.shape
    t = BAND_TILE
    nblk = s // t
    wblk = min(nblk, 2 * BAND_REACH // t + 1)
    hw = heads * HEAD_DIM
    assert s % (qsub * t) == 0 and H_A % heads == 0 and OFF_KA % hw == 0 and OFF_VA % hw == 0
    kv_spec = lambda off: pl.BlockSpec((None, s, hw), lambda bi, h, qi: (bi, 0, off // hw + h),
                                       pipeline_mode=pl.Buffered(1))
    return pl.pallas_call(
        functools.partial(_attn_a_kernel, nblk=nblk, wblk=wblk, heads=heads, qsub=qsub),
        grid=(b, H_A // heads, nblk // qsub),
        in_specs=[pl.BlockSpec((None, qsub * t, hw), lambda bi, h, qi: (bi, qi, OFF_QA // hw + h)),
                  kv_spec(OFF_KA), kv_spec(OFF_VA),
                  pl.BlockSpec(band_tbl.shape, lambda bi, h, qi: (0, 0, 0), pipeline_mode=pl.Buffered(1))],
        out_specs=pl.BlockSpec((None, qsub * t, hw), lambda bi, h, qi: (bi, qi, h)),
        out_shape=jax.ShapeDtypeStruct((b, s, W_A), BF16),
        scratch_shapes=[pltpu.VMEM((qsub * heads, wblk * t, 2 * HEAD_DIM), BF16)],
        compiler_params=_params(("parallel", "parallel", "arbitrary")),
        name="attn_dilated",
    )(proj3, proj3, proj3, band_tbl)


def _rpb_table_kernel(rpb_ref, o_ref):
    h = pl.program_id(0)
    a = pl.program_id(1) - 1
    n_roff, n_coff = 2 * NA_ROWS - 1, 2 * NA_COLS - 1
    shape = (GRID_W, 2 * GRID_W)
    qc = lax.broadcasted_iota(jnp.int32, shape, 0)
    lane = lax.broadcasted_iota(jnp.int32, shape, 1)
    hi = lane >= GRID_W
    kc = jnp.where(hi, lane - GRID_W, lane)
    coff = jnp.clip(kc - qc, -(NA_COLS - 1), NA_COLS - 1) + (NA_COLS - 1)
    cstart = jnp.clip(qc - NA_COLS // 2, 0, GRID_W - NA_COLS)
    cmask = (kc >= cstart) & (kc < cstart + NA_COLS)
    a_lo = jnp.clip(a, 0, n_roff - 1)
    a_hi = jnp.clip(a + 1, 0, n_roff - 1)
    val = jnp.zeros(shape, F32)
    for bb in range(n_coff):
        r_lo = rpb_ref[(h * n_roff + a_lo) * n_coff + bb]
        r_hi = rpb_ref[(h * n_roff + a_hi) * n_coff + bb]
        val = jnp.where(coff == bb, jnp.where(hi, r_hi, r_lo), val)
    ok_lo = ((a >= 0) & (a < n_roff)).astype(jnp.int32)
    ok_hi = ((a + 1 >= 0) & (a + 1 < n_roff)).astype(jnp.int32)
    ok = jnp.where(hi, ok_hi, ok_lo) > 0
    o_ref[0, 0] = jnp.where(cmask & ok, val * LOG2_E, NEG_INF)


def _rpb_table(rpb):
    h = rpb.shape[0]
    n_slots = 2 * NA_ROWS
    return pl.pallas_call(
        _rpb_table_kernel,
        grid=(h, n_slots),
        in_specs=[pl.BlockSpec(memory_space=pltpu.SMEM)],
        out_specs=pl.BlockSpec((1, 1, GRID_W, 2 * GRID_W), lambda hi, ai: (hi, ai, 0, 0)),
        out_shape=jax.ShapeDtypeStruct((h, n_slots, GRID_W, 2 * GRID_W), F32),
        compiler_params=_params(("arbitrary", "arbitrary")),
        name="rpb_table",
    )(rpb.reshape(-1))


def _attn_c_kernel(q_ref, k_ref, v_ref, tc_ref, o_ref, vx_ref, *, rows, heads):
    b = pl.program_id(2)
    nb = rows // NA_QROWS
    kblk = NA_KROWS // NA_QROWS
    kb = jnp.clip(b - 1, 0, nb - kblk)
    nk = NA_KROWS * GRID_W
    win = pl.ds(pl.multiple_of(kb * NA_TILE, NA_TILE), nk)

    @pl.when(b == 0)
    def _():
        vx_ref[:, :, HEAD_DIM:] = jnp.ones((heads, nk, HEAD_DIM), vx_ref.dtype)

    lane = lax.broadcasted_iota(jnp.int32, (GRID_W, 2 * GRID_W), 1)
    for hh in range(heads):
        hs = slice(hh * HEAD_DIM, (hh + 1) * HEAD_DIM)
        vx_ref[hh, :, :HEAD_DIM] = v_ref[win, hs]
        strips = []
        for ql in range(NA_QROWS):
            i = b * NA_QROWS + ql
            rstart = jnp.clip(i - NA_ROWS // 2, 0, rows - NA_ROWS)
            pieces = []
            for pr in range(NA_KROWS // 2):
                kr = kb * NA_QROWS + 2 * pr
                a = kr - i + (NA_ROWS - 1)
                v_lo = ((kr >= rstart) & (kr < rstart + NA_ROWS)).astype(jnp.int32)
                v_hi = ((kr + 1 >= rstart) & (kr + 1 < rstart + NA_ROWS)).astype(jnp.int32)
                valid = jnp.where(lane < GRID_W, v_lo, v_hi) > 0
                tb = tc_ref[hh, jnp.clip(a + 1, 0, 2 * NA_ROWS - 1)]
                pieces.append(jnp.where(valid, tb, NEG_INF))
            strips.append(jnp.concatenate(pieces, axis=1))
        s = _qk(q_ref[:, hs], k_ref[win, hs]) + jnp.concatenate(strips, axis=0)
        m = functools.reduce(jnp.maximum, [s[:, c * LANES:(c + 1) * LANES] for c in range(nk // LANES)])
        m = jnp.broadcast_to(jnp.max(m, axis=-1, keepdims=True), (NA_TILE, LANES))
        p = _exp2_rows(s, m).astype(vx_ref.dtype)
        pv = jnp.dot(p, vx_ref[hh], preferred_element_type=F32)
        o_ref[:, hs] = (pv[:, :HEAD_DIM] / pv[:, HEAD_DIM:]).astype(o_ref.dtype)


def _attn_c(proj3, tc, heads=2):
    b, s, _ = proj3.shape
    rows = s // GRID_W
    hw = heads * HEAD_DIM
    assert s % GRID_W == 0 and rows % NA_QROWS == 0 and rows >= NA_KROWS
    assert H_C % heads == 0 and OFF_QC % hw == 0 and OFF_KC % hw == 0 and OFF_VC % hw == 0
    kv_spec = lambda off: pl.BlockSpec((None, s, hw), lambda bi, h, qi: (bi, 0, off // hw + h),
                                       pipeline_mode=pl.Buffered(1))
    return pl.pallas_call(
        functools.partial(_attn_c_kernel, rows=rows, heads=heads),
        grid=(b, H_C // heads, s // NA_TILE),
        in_specs=[pl.BlockSpec((None, NA_TILE, hw), lambda bi, h, qi: (bi, qi, OFF_QC // hw + h)),
                  kv_spec(OFF_KC), kv_spec(OFF_VC),
                  pl.BlockSpec((heads, 2 * NA_ROWS, GRID_W, 2 * GRID_W), lambda bi, h, qi: (h, 0, 0, 0))],
        out_specs=pl.BlockSpec((None, NA_TILE, hw), lambda bi, h, qi: (bi, qi, h)),
        out_shape=jax.ShapeDtypeStruct((b, s, W_C), BF16),
        scratch_shapes=[pltpu.VMEM((heads, NA_KROWS * GRID_W, 2 * HEAD_DIM), BF16)],
        compiler_params=_params(("parallel", "parallel", "arbitrary")),
        name="attn_neighbourhood",
    )(proj3, proj3, proj3, tc)


def _layer(x, xb, b, s, lw, tabs, band_tbl):
    m, d = x.shape
    proj = _in_proj(xb, lw["w_in"], tabs, lw["g_qn"], lw["g_kn"], s)
    proj3 = proj.reshape(b, s, IN_WIDTH)
    oa = _attn_a(proj3, band_tbl).reshape(m, W_A)
    ob = _attn_b(proj3).reshape(m, W_BQ)
    oc = _attn_c(proj3, lw["tc"]).reshape(m, W_C)
    mixed = _mix(oa, ob, oc, lw["g_grp"])
    y = _mm_resid(mixed, lw["w_out"], x, lw["alpha"], tm=1024)
    x, xb = _layer_norm(y, lw["ln1_g"], lw["ln1_b"])
    hdn = _gate_up(xb, lw["w_gate"], lw["w_up"], tm=1024)
    y = _mm_resid(hdn, lw["w_down"], x, lw["alpha"], tm=512)
    return _layer_norm(y, lw["ln2_g"], lw["ln2_b"])


def kernel(x_prompt, x_sample, w_in, g_qn, g_kn, rpb, g_grp, w_out, ln1_g, ln1_b, w_gate, w_up, w_down,
           ln2_g, ln2_b):
    depth = w_in.shape[0]
    layers = []
    for l in range(depth):
        layers.append(dict(
            w_in=_cast_tiled(w_in, l, 512), w_out=_cast_tiled(w_out, l, 512),
            w_gate=_cast_tiled(w_gate, l, 256), w_up=_cast_tiled(w_up, l, 256),
            w_down=_cast_tiled(w_down, l, 256),
            g_qn=g_qn[l].reshape(1, HEAD_DIM), g_kn=g_kn[l].reshape(1, HEAD_DIM),
            tc=_rpb_table(rpb[l]), g_grp=g_grp[l], alpha=(2.0 * depth) ** 0.25,
            ln1_g=ln1_g[l], ln1_b=ln1_b[l], ln2_g=ln2_g[l], ln2_b=ln2_b[l]))
    band_tbl = jnp.asarray(_band_bias_table())

    def trunk(x3):
        b, s, d = x3.shape
        tabs = _rope_tables(s)
        x = x3.reshape(b * s, d)
        xb = x.astype(BF16)
        for lw in layers:
            x, xb = _layer(x, xb, b, s, lw, tabs, band_tbl)
        return x.reshape(b, s, d)

    return (trunk(x_prompt), trunk(x_sample))
```

```python
import functools

import numpy as np
import jax
import jax.numpy as jnp
from jax import lax
from jax.experimental import pallas as pl
from jax.experimental.pallas import tpu as pltpu

F32 = jnp.float32
BF16 = jnp.bfloat16

HEAD_DIM = 128
H_A, H_B, HKV_B, H_C = 8, 16, 4, 8
GRP_B = H_B // HKV_B
DILATED_PATTERNS = ((128, 1), (512, 4), (2048, 16))
ROPE_THETA = 500000.0
ROPE_DIM = HEAD_DIM // 4
AXIAL_THETA = 10000.0
GRID_W = 64
NA_ROWS = 8
NA_COLS = 16
LN_EPS = 1e-5
RMS_EPS = 1e-6
NEG_INF = -1e30
LOG2_E = 1.4426950408889634
Q_SCALE = HEAD_DIM ** -0.5 * LOG2_E

W_A = H_A * HEAD_DIM
W_BQ = H_B * HEAD_DIM
W_BKV = HKV_B * HEAD_DIM
W_C = H_C * HEAD_DIM
OFF_QA, OFF_KA, OFF_VA = 0, W_A, 2 * W_A
OFF_QB = 3 * W_A
OFF_KB = OFF_QB + W_BQ
OFF_VB = OFF_KB + W_BKV
OFF_QC = OFF_VB + W_BKV
OFF_KC = OFF_QC + W_C
OFF_VC = OFF_KC + W_C
IN_WIDTH = OFF_VC + W_C

LANES = 128
VMEM_LIMIT_BYTES = 52 * 1024 * 1024

BAND_REACH = max(w // 2 for w, _ in DILATED_PATTERNS)
BAND_TILE = 256
NA_QROWS = 4
NA_TILE = NA_QROWS * GRID_W
NA_KROWS = 12


def _params(sem):
    return pltpu.CompilerParams(dimension_semantics=sem, vmem_limit_bytes=VMEM_LIMIT_BYTES)


def _rot(h, c, s_up, s_dn, shift):
    return h * c + pltpu.roll(h, LANES - shift, 1) * s_up + pltpu.roll(h, shift, 1) * s_dn


def _inproj_kernel(x_ref, w_ref, ca_ref, ua_ref, da_ref, cb_ref, ub_ref, db_ref, gq_ref, gk_ref,
                   o_ref, *, tn, nchunk):
    j = pl.program_id(1)
    rc = x_ref.shape[0] // nchunk
    heads = tn // HEAD_DIM
    is_rope_a = j < OFF_VA // tn
    is_qb = (j >= OFF_QB // tn) & (j < OFF_KB // tn)
    is_kb = (j >= OFF_KB // tn) & (j < OFF_VB // tn)
    is_plain = jnp.logical_not(is_rope_a | is_qb | is_kb)
    is_q = (j < OFF_KA // tn) | is_qb | ((j >= OFF_QC // tn) & (j < OFF_KC // tn))
    qs = jnp.where(is_q, Q_SCALE, 1.0).astype(F32)

    def run(epilogue):
        for c in range(nchunk):
            rows = slice(c * rc, (c + 1) * rc)
            acc = jnp.dot(x_ref[rows, :], w_ref[...], preferred_element_type=F32)
            for hh in range(heads):
                sl = slice(hh * HEAD_DIM, (hh + 1) * HEAD_DIM)
                o_ref[rows, sl] = (epilogue(acc[:, sl], rows) * qs).astype(o_ref.dtype)

    def rope_a(h, rows):
        return _rot(h, ca_ref[rows, :], ua_ref[rows, :], da_ref[rows, :], ROPE_DIM // 2)

    def norm_rope_b(g_ref):
        def f(h, rows):
            h = h * lax.rsqrt(jnp.mean(h * h, axis=-1, keepdims=True) + RMS_EPS) * g_ref[...]
            return _rot(h, cb_ref[rows, :], ub_ref[rows, :], db_ref[rows, :], HEAD_DIM // 4)
        return f

    pl.when(is_plain)(lambda: run(lambda h, rows: h))
    pl.when(is_rope_a)(lambda: run(rope_a))
    pl.when(is_qb)(lambda: run(norm_rope_b(gq_ref)))
    pl.when(is_kb)(lambda: run(norm_rope_b(gk_ref)))


def _w_spec(w):
    _, k, tn = w.shape
    return pl.BlockSpec((None, k, tn), lambda i, j: (j, 0, 0))


def _in_proj(xb, w, tabs, gq, gk, seq, tm=1024, nchunk=4):
    m, k = xb.shape
    nt, _, tn = w.shape
    assert m % tm == 0 and seq % tm == 0 and tm % nchunk == 0
    for off in (OFF_KA, OFF_VA, OFF_QB, OFF_KB, OFF_VB, OFF_QC, OFF_KC):
        assert off % tn == 0
    nseq = seq // tm
    tab_spec = pl.BlockSpec((tm, LANES), lambda i, j: (i % nseq, 0))
    g_spec = pl.BlockSpec((1, LANES), lambda i, j: (0, 0))
    return pl.pallas_call(
        functools.partial(_inproj_kernel, tn=tn, nchunk=nchunk),
        grid=(m // tm, nt),
        in_specs=[pl.BlockSpec((tm, k), lambda i, j: (i, 0)), _w_spec(w)] + [tab_spec] * 6 + [g_spec, g_spec],
        out_specs=pl.BlockSpec((tm, tn), lambda i, j: (i, j)),
        out_shape=jax.ShapeDtypeStruct((m, nt * tn), BF16),
        compiler_params=_params(("parallel", "arbitrary")),
        name="in_proj",
    )(xb, w, *tabs, gq, gk)


def _cast_kernel(w_ref, o_ref):
    o_ref[...] = w_ref[...].astype(o_ref.dtype)


def _cast_tiled(w3, layer, tn):
    _, k, n = w3.shape
    assert n % tn == 0
    return pl.pallas_call(
        _cast_kernel,
        grid=(n // tn,),
        in_specs=[pl.BlockSpec((None, k, tn), lambda j: (layer, 0, j))],
        out_specs=pl.BlockSpec((None, k, tn), lambda j: (j, 0, 0)),
        out_shape=jax.ShapeDtypeStruct((n // tn, k, tn), BF16),
        compiler_params=_params(("parallel",)),
        name="cast_weight",
    )(w3)


def _rope_tables(seq):
    def cs(pos, theta, d):
        half = d // 2
        inv = theta ** (-jnp.arange(half, dtype=F32) * 2.0 / d)
        ang = pos.astype(F32)[:, None] * inv[None, :]
        return jnp.cos(ang), jnp.sin(ang)

    t = jnp.arange(seq)
    ca, sa = cs(t, ROPE_THETA, ROPE_DIM)
    pad = HEAD_DIM - ROPE_DIM
    z = lambda w_: jnp.zeros((seq, w_), F32)
    tab_ca = jnp.concatenate([ca, ca, jnp.ones((seq, pad), F32)], 1)
    tab_ua = jnp.concatenate([-sa, z(ROPE_DIM // 2 + pad)], 1)
    tab_da = jnp.concatenate([z(ROPE_DIM // 2), sa, z(pad)], 1)
    cr, sr = cs(t // GRID_W, AXIAL_THETA, HEAD_DIM // 2)
    cc, sc = cs(t % GRID_W, AXIAL_THETA, HEAD_DIM // 2)
    q = HEAD_DIM // 4
    tab_cb = jnp.concatenate([cr, cr, cc, cc], 1)
    tab_ub = jnp.concatenate([-sr, z(q), -sc, z(q)], 1)
    tab_db = jnp.concatenate([z(q), sr, z(q), sc], 1)
    return (tab_ca, tab_ua, tab_da, tab_cb, tab_ub, tab_db)


def _mm_resid_kernel(x_ref, w_ref, r_ref, o_ref, *, alpha):
    o_ref[...] = alpha * r_ref[...] + jnp.dot(x_ref[...], w_ref[...], preferred_element_type=F32)


def _ln_apply(y, mu, rs, g, b):
    reps = y.shape[1] // LANES
    mu = jnp.concatenate([mu] * reps, axis=1)
    rs = jnp.concatenate([rs] * reps, axis=1)
    return (y - mu) * rs * g + b


def _mm_resid_ln_kernel(x_ref, w_ref, y_ref, mu_ref, rs_ref, g_ref, b_ref, o_ref, *, alpha):
    resid = _ln_apply(y_ref[...], mu_ref[...], rs_ref[...], g_ref[...], b_ref[...])
    o_ref[...] = alpha * resid + jnp.dot(x_ref[...], w_ref[...], preferred_element_type=F32)


def _mm_resid(xb, w, resid, alpha, tm):
    m, k = xb.shape
    nt, _, tn = w.shape
    assert m % tm == 0
    tile = pl.BlockSpec((tm, tn), lambda i, j: (i, j))
    if isinstance(resid, tuple):
        y, mu, rs, g, b = resid
        stat = pl.BlockSpec((tm, LANES), lambda i, j: (i, 0))
        vec = pl.BlockSpec((1, tn), lambda i, j: (0, j))
        body = _mm_resid_ln_kernel
        extra, extra_specs = (y, mu, rs, g.reshape(1, -1), b.reshape(1, -1)), [tile, stat, stat, vec, vec]
    else:
        body = _mm_resid_kernel
        extra, extra_specs = (resid,), [tile]
    return pl.pallas_call(
        functools.partial(body, alpha=alpha),
        grid=(m // tm, nt),
        in_specs=[pl.BlockSpec((tm, k), lambda i, j: (i, 0)), _w_spec(w)] + extra_specs,
        out_specs=tile,
        out_shape=jax.ShapeDtypeStruct((m, nt * tn), F32),
        compiler_params=_params(("parallel", "arbitrary")),
        name="mm_resid",
    )(xb, w, *extra)


def _gateup_kernel(x_ref, wg_ref, wu_ref, o_ref):
    x = x_ref[...]
    g = jnp.dot(x, wg_ref[...], preferred_element_type=F32)
    u = jnp.dot(x, wu_ref[...], preferred_element_type=F32)
    o_ref[...] = (g / (1.0 + jnp.exp(-g)) * u).astype(o_ref.dtype)


def _gate_up(xb, wg, wu, tm):
    m, k = xb.shape
    nt, _, tn = wg.shape
    assert m % tm == 0 and wu.shape == wg.shape
    return pl.pallas_call(
        _gateup_kernel,
        grid=(m // tm, nt),
        in_specs=[pl.BlockSpec((tm, k), lambda i, j: (i, 0)), _w_spec(wg), _w_spec(wu)],
        out_specs=pl.BlockSpec((tm, tn), lambda i, j: (i, j)),
        out_shape=jax.ShapeDtypeStruct((m, nt * tn), BF16),
        compiler_params=_params(("parallel", "arbitrary")),
        name="gate_up",
    )(xb, wg, wu)


def _ln_stats(y):
    mu = jnp.mean(y, axis=-1, keepdims=True)
    d = y - mu
    var = jnp.mean(d * d, axis=-1, keepdims=True)
    return mu, lax.rsqrt(var + LN_EPS)


def _ln_final_kernel(y_ref, g_ref, b_ref, o_ref):
    y = y_ref[...]
    mu, rs = _ln_stats(y)
    o_ref[...] = (y - mu) * rs * g_ref[...] + b_ref[...]


def _ln_bf16_kernel(y_ref, g_ref, b_ref, ob_ref, mu_ref, rs_ref):
    y = y_ref[...]
    mu, rs = _ln_stats(y)
    ob_ref[...] = ((y - mu) * rs * g_ref[...] + b_ref[...]).astype(ob_ref.dtype)
    mu_ref[...] = jnp.broadcast_to(mu, mu_ref.shape)
    rs_ref[...] = jnp.broadcast_to(rs, rs_ref.shape)


def _layer_norm(y, g, b, final, tm=256):
    m, d = y.shape
    row = pl.BlockSpec((tm, d), lambda i: (i, 0))
    vec = pl.BlockSpec((1, d), lambda i: (0, 0))
    stat = pl.BlockSpec((tm, LANES), lambda i: (i, 0))
    if final:
        body, out_specs, out_shape = _ln_final_kernel, row, jax.ShapeDtypeStruct((m, d), F32)
    else:
        body, out_specs = _ln_bf16_kernel, [row, stat, stat]
        out_shape = [jax.ShapeDtypeStruct((m, d), BF16)] + [jax.ShapeDtypeStruct((m, LANES), F32)] * 2
    return pl.pallas_call(
        body,
        grid=(m // tm,),
        in_specs=[row, vec, vec],
        out_specs=out_specs,
        out_shape=out_shape,
        compiler_params=_params(("parallel",)),
        name="layer_norm",
    )(y, g.reshape(1, d), b.reshape(1, d))


def _mix_kernel(oa_ref, ob_ref, oc_ref, g_ref, o_ref):
    off = 0
    for ref in (oa_ref, ob_ref, oc_ref):
        w = ref.shape[1]
        x = ref[...].astype(F32)
        xn = x * lax.rsqrt(jnp.mean(x * x, axis=-1, keepdims=True) + RMS_EPS)
        o_ref[:, off:off + w] = (xn * g_ref[:, off:off + w]).astype(o_ref.dtype)
        off += w


def _mix(oa, ob, oc, g, tm=512):
    m = oa.shape[0]
    d = oa.shape[1] + ob.shape[1] + oc.shape[1]
    return pl.pallas_call(
        _mix_kernel,
        grid=(m // tm,),
        in_specs=[pl.BlockSpec((tm, oa.shape[1]), lambda i: (i, 0)),
                  pl.BlockSpec((tm, ob.shape[1]), lambda i: (i, 0)),
                  pl.BlockSpec((tm, oc.shape[1]), lambda i: (i, 0)),
                  pl.BlockSpec((1, d), lambda i: (0, 0))],
        out_specs=pl.BlockSpec((tm, d), lambda i: (i, 0)),
        out_shape=jax.ShapeDtypeStruct((m, d), BF16),
        compiler_params=_params(("parallel",)),
        name="group_rms_mix",
    )(oa, ob, oc, g.reshape(1, d))


def _qk(q, k):
    return lax.dot_general(q, k, (((1,), (1,)), ((), ())), preferred_element_type=F32)


def _exp2_rows(s, m):
    return jnp.concatenate([jnp.exp2(s[:, c * LANES:(c + 1) * LANES] - m) for c in range(s.shape[1] // LANES)],
                           axis=1)


def _attn_b_kernel(q_ref, k_ref, v_ref, o_ref, m_ref, l_ref, acc_ref, vx_ref, *, chain):
    kv = pl.program_id(3)
    tq = q_ref.shape[0]
    tk = k_ref.shape[0]

    @pl.when(kv == 0)
    def _():
        m_ref[...] = jnp.full(m_ref.shape, -jnp.inf, F32)
        l_ref[...] = jnp.zeros(l_ref.shape, F32)
        acc_ref[...] = jnp.zeros(acc_ref.shape, F32)
        vx_ref[:, HEAD_DIM:] = jnp.ones((tk, HEAD_DIM), vx_ref.dtype)

    vx_ref[:, :HEAD_DIM] = v_ref[...]
    k = k_ref[...]
    vx = vx_ref[...]
    for r in range(GRP_B):
        for c in range(tq // chain):
            rows = slice(r * tq + c * chain, r * tq + (c + 1) * chain)
            s = _qk(q_ref[c * chain:(c + 1) * chain, r * HEAD_DIM:(r + 1) * HEAD_DIM], k)
            m_prev = m_ref[rows]
            m_new = jnp.maximum(m_prev, jnp.max(s, axis=-1, keepdims=True))
            alpha = jnp.exp2(m_prev - m_new)
            p = _exp2_rows(s, m_new)
            pv = jnp.dot(p.astype(vx.dtype), vx, preferred_element_type=F32)
            acc_ref[rows] = alpha * acc_ref[rows] + pv[:, :HEAD_DIM]
            l_ref[rows] = alpha * l_ref[rows] + pv[:, HEAD_DIM:]
            m_ref[rows] = m_new

    @pl.when(kv == pl.num_programs(3) - 1)
    def _():
        o = acc_ref[...] / l_ref[...]
        for r in range(GRP_B):
            o_ref[:, r * HEAD_DIM:(r + 1) * HEAD_DIM] = o[r * tq:(r + 1) * tq].astype(o_ref.dtype)


def _attn_b(proj3, tq=2048, tk=2048, chain=256):
    b, s, _ = proj3.shape
    qw = GRP_B * HEAD_DIM
    assert s % tq == 0 and s % tk == 0 and tq % chain == 0 and OFF_QB % qw == 0
    stat = pltpu.VMEM((GRP_B * tq, HEAD_DIM), F32)
    return pl.pallas_call(
        functools.partial(_attn_b_kernel, chain=chain),
        grid=(b, HKV_B, s // tq, s // tk),
        in_specs=[pl.BlockSpec((None, tq, qw), lambda bi, g, qi, ki: (bi, qi, OFF_QB // qw + g)),
                  pl.BlockSpec((None, tk, HEAD_DIM), lambda bi, g, qi, ki: (bi, ki, OFF_KB // HEAD_DIM + g)),
                  pl.BlockSpec((None, tk, HEAD_DIM), lambda bi, g, qi, ki: (bi, ki, OFF_VB // HEAD_DIM + g))],
        out_specs=pl.BlockSpec((None, tq, qw), lambda bi, g, qi, ki: (bi, qi, g)),
        out_shape=jax.ShapeDtypeStruct((b, s, W_BQ), BF16),
        scratch_shapes=[stat, stat, stat, pltpu.VMEM((tk, 2 * HEAD_DIM), BF16)],
        compiler_params=_params(("parallel", "parallel", "parallel", "arbitrary")),
        name="attn_axial_gqa",
    )(proj3, proj3, proj3)


def _band_bias_table():
    nt = 2 * BAND_REACH // BAND_TILE + 1
    u = np.arange(nt + 1)[:, None, None] - BAND_REACH // BAND_TILE
    rel = u * BAND_TILE + np.arange(BAND_TILE)[None, None, :] - np.arange(BAND_TILE)[None, :, None]
    mult = np.zeros(rel.shape, np.int64)
    for window, dil in DILATED_PATTERNS:
        mult += (np.abs(rel) <= window // 2) & (rel % dil == 0)
    mult[nt] = 0
    return np.where(mult > 0, np.log2(np.maximum(mult, 1)), NEG_INF).astype(np.float32)


def _attn_a_kernel(q_ref, k_ref, v_ref, tbl_ref, o_ref, vx_ref, *, nblk, wblk, heads, qsub):
    t = BAND_TILE
    r = BAND_REACH // t
    nt = 2 * r + 1
    qi = pl.program_id(2)

    @pl.when(qi == 0)
    def _():
        vx_ref[:, :, HEAD_DIM:] = jnp.ones((qsub * heads, wblk * t, HEAD_DIM), vx_ref.dtype)

    for u in range(qsub):
        qt = qi * qsub + u
        sb = jnp.clip(qt - r, 0, nblk - wblk)
        idx = []
        for c in range(wblk):
            d = sb + c - qt + r
            idx.append(jnp.where((d >= 0) & (d < nt), d, nt))
        qrows = slice(u * t, (u + 1) * t)
        for hh in range(heads):
            ch = u * heads + hh
            hs = slice(hh * HEAD_DIM, (hh + 1) * HEAD_DIM)
            q = q_ref[qrows, hs]
            s = []
            for c in range(wblk):
                rows = pl.ds(pl.multiple_of((sb + c) * t, t), t)
                vx_ref[ch, c * t:(c + 1) * t, :HEAD_DIM] = v_ref[rows, hs]
                s.append(_qk(q, k_ref[rows, hs]) + tbl_ref[idx[c]])
            m = functools.reduce(jnp.maximum, s)
            m = functools.reduce(jnp.maximum, [m[:, c * LANES:(c + 1) * LANES] for c in range(t // LANES)])
            m = jnp.broadcast_to(jnp.max(m, axis=-1, keepdims=True), (t, LANES))
            pv = jnp.zeros((t, 2 * HEAD_DIM), F32)
            for c in range(wblk):
                p = _exp2_rows(s[c], m).astype(vx_ref.dtype)
                pv = pv + jnp.dot(p, vx_ref[ch, c * t:(c + 1) * t, :], preferred_element_type=F32)
            o_ref[qrows, hs] = (pv[:, :HEAD_DIM] / pv[:, HEAD_DIM:]).astype(o_ref.dtype)


def _attn_a(proj3, band_tbl, heads=2, qsub=4):
    b, s, _ = proj3.shape
    t = BAND_TILE
    nblk = s // t
    wblk = min(nblk, 2 * BAND_REACH // t + 1)
    hw = heads * HEAD_DIM
    assert s % (qsub * t) == 0 and H_A % heads == 0 and OFF_KA % hw == 0 and OFF_VA % hw == 0
    kv_spec = lambda off: pl.BlockSpec((None, s, hw), lambda bi, h, qi: (bi, 0, off // hw + h),
                                       pipeline_mode=pl.Buffered(1))
    return pl.pallas_call(
        functools.partial(_attn_a_kernel, nblk=nblk, wblk=wblk, heads=heads, qsub=qsub),
        grid=(b, H_A // heads, nblk // qsub),
        in_specs=[pl.BlockSpec((None, qsub * t, hw), lambda bi, h, qi: (bi, qi, OFF_QA // hw + h)),
                  kv_spec(OFF_KA), kv_spec(OFF_VA),
                  pl.BlockSpec(band_tbl.shape, lambda bi, h, qi: (0, 0, 0), pipeline_mode=pl.Buffered(1))],
        out_specs=pl.BlockSpec((None, qsub * t, hw), lambda bi, h, qi: (bi, qi, h)),
        out_shape=jax.ShapeDtypeStruct((b, s, W_A), BF16),
        scratch_shapes=[pltpu.VMEM((qsub * heads, wblk * t, 2 * HEAD_DIM), BF16)],
        compiler_params=_params(("parallel", "parallel", "arbitrary")),
        name="attn_dilated",
    )(proj3, proj3, proj3, band_tbl)


def _rpb_table_kernel(rpb_ref, o_ref):
    h = pl.program_id(0)
    a = pl.program_id(1) - 1
    n_roff, n_coff = 2 * NA_ROWS - 1, 2 * NA_COLS - 1
    shape = (GRID_W, 2 * GRID_W)
    qc = lax.broadcasted_iota(jnp.int32, shape, 0)
    lane = lax.broadcasted_iota(jnp.int32, shape, 1)
    hi = lane >= GRID_W
    kc = jnp.where(hi, lane - GRID_W, lane)
    coff = jnp.clip(kc - qc, -(NA_COLS - 1), NA_COLS - 1) + (NA_COLS - 1)
    cstart = jnp.clip(qc - NA_COLS // 2, 0, GRID_W - NA_COLS)
    cmask = (kc >= cstart) & (kc < cstart + NA_COLS)
    a_lo = jnp.clip(a, 0, n_roff - 1)
    a_hi = jnp.clip(a + 1, 0, n_roff - 1)
    val = jnp.zeros(shape, F32)
    for bb in range(n_coff):
        r_lo = rpb_ref[(h * n_roff + a_lo) * n_coff + bb]
        r_hi = rpb_ref[(h * n_roff + a_hi) * n_coff + bb]
        val = jnp.where(coff == bb, jnp.where(hi, r_hi, r_lo), val)
    ok_lo = ((a >= 0) & (a < n_roff)).astype(jnp.int32)
    ok_hi = ((a + 1 >= 0) & (a + 1 < n_roff)).astype(jnp.int32)
    ok = jnp.where(hi, ok_hi, ok_lo) > 0
    o_ref[0, 0] = jnp.where(cmask & ok, val * LOG2_E, NEG_INF)


def _rpb_table(rpb):
    h = rpb.shape[0]
    n_slots = 2 * NA_ROWS
    return pl.pallas_call(
        _rpb_table_kernel,
        grid=(h, n_slots),
        in_specs=[pl.BlockSpec(memory_space=pltpu.SMEM)],
        out_specs=pl.BlockSpec((1, 1, GRID_W, 2 * GRID_W), lambda hi, ai: (hi, ai, 0, 0)),
        out_shape=jax.ShapeDtypeStruct((h, n_slots, GRID_W, 2 * GRID_W), F32),
        compiler_params=_params(("arbitrary", "arbitrary")),
        name="rpb_table",
    )(rpb.reshape(-1))


def _attn_c_kernel(q_ref, k_ref, v_ref, tc_ref, o_ref, vx_ref, *, rows, heads, qsub):
    nb = rows // NA_QROWS
    kblk = NA_KROWS // NA_QROWS
    nk = NA_KROWS * GRID_W

    @pl.when(pl.program_id(2) == 0)
    def _():
        vx_ref[:, :, HEAD_DIM:] = jnp.ones((qsub * heads, nk, HEAD_DIM), vx_ref.dtype)

    lane = lax.broadcasted_iota(jnp.int32, (GRID_W, 2 * GRID_W), 1)
    for u in range(qsub):
        b = pl.program_id(2) * qsub + u
        kb = jnp.clip(b - 1, 0, nb - kblk)
        win = pl.ds(pl.multiple_of(kb * NA_TILE, NA_TILE), nk)
        qrows = slice(u * NA_TILE, (u + 1) * NA_TILE)
        sel = []
        for ql in range(NA_QROWS):
            i = b * NA_QROWS + ql
            rstart = jnp.clip(i - NA_ROWS // 2, 0, rows - NA_ROWS)
            for pr in range(NA_KROWS // 2):
                kr = kb * NA_QROWS + 2 * pr
                a = kr - i + (NA_ROWS - 1)
                v_lo = ((kr >= rstart) & (kr < rstart + NA_ROWS)).astype(jnp.int32)
                v_hi = ((kr + 1 >= rstart) & (kr + 1 < rstart + NA_ROWS)).astype(jnp.int32)
                sel.append((jnp.where(lane < GRID_W, v_lo, v_hi) > 0, jnp.clip(a + 1, 0, 2 * NA_ROWS - 1)))
        for hh in range(heads):
            ch = u * heads + hh
            hs = slice(hh * HEAD_DIM, (hh + 1) * HEAD_DIM)
            vx_ref[ch, :, :HEAD_DIM] = v_ref[win, hs]
            strips = []
            for ql in range(NA_QROWS):
                pieces = []
                for pr in range(NA_KROWS // 2):
                    valid, slot = sel[ql * (NA_KROWS // 2) + pr]
                    pieces.append(jnp.where(valid, tc_ref[hh, slot], NEG_INF))
                strips.append(jnp.concatenate(pieces, axis=1))
            s = _qk(q_ref[qrows, hs], k_ref[win, hs]) + jnp.concatenate(strips, axis=0)
            m = functools.reduce(jnp.maximum, [s[:, c * LANES:(c + 1) * LANES] for c in range(nk // LANES)])
            m = jnp.broadcast_to(jnp.max(m, axis=-1, keepdims=True), (NA_TILE, LANES))
            p = _exp2_rows(s, m).astype(vx_ref.dtype)
            pv = jnp.dot(p, vx_ref[ch], preferred_element_type=F32)
            o_ref[qrows, hs] = (pv[:, :HEAD_DIM] / pv[:, HEAD_DIM:]).astype(o_ref.dtype)


def _attn_c(proj3, tc, heads=2, qsub=8):
    b, s, _ = proj3.shape
    rows = s // GRID_W
    hw = heads * HEAD_DIM
    tq = qsub * NA_TILE
    assert s % tq == 0 and rows % NA_QROWS == 0 and rows >= NA_KROWS
    assert H_C % heads == 0 and OFF_QC % hw == 0 and OFF_KC % hw == 0 and OFF_VC % hw == 0
    kv_spec = lambda off: pl.BlockSpec((None, s, hw), lambda bi, h, qi: (bi, 0, off // hw + h),
                                       pipeline_mode=pl.Buffered(1))
    return pl.pallas_call(
        functools.partial(_attn_c_kernel, rows=rows, heads=heads, qsub=qsub),
        grid=(b, H_C // heads, s // tq),
        in_specs=[pl.BlockSpec((None, tq, hw), lambda bi, h, qi: (bi, qi, OFF_QC // hw + h)),
                  kv_spec(OFF_KC), kv_spec(OFF_VC),
                  pl.BlockSpec((heads, 2 * NA_ROWS, GRID_W, 2 * GRID_W), lambda bi, h, qi: (h, 0, 0, 0))],
        out_specs=pl.BlockSpec((None, tq, hw), lambda bi, h, qi: (bi, qi, h)),
        out_shape=jax.ShapeDtypeStruct((b, s, W_C), BF16),
        scratch_shapes=[pltpu.VMEM((qsub * heads, NA_KROWS * GRID_W, 2 * HEAD_DIM), BF16)],
        compiler_params=_params(("parallel", "parallel", "arbitrary")),
        name="attn_neighbourhood",
    )(proj3, proj3, proj3, tc)


def _layer(x, xb, b, s, lw, tabs, band_tbl, last):
    m = xb.shape[0]
    proj = _in_proj(xb, lw["w_in"], tabs, lw["g_qn"], lw["g_kn"], s)
    proj3 = proj.reshape(b, s, IN_WIDTH)
    oa = _attn_a(proj3, band_tbl).reshape(m, W_A)
    ob = _attn_b(proj3).reshape(m, W_BQ)
    oc = _attn_c(proj3, lw["tc"]).reshape(m, W_C)
    mixed = _mix(oa, ob, oc, lw["g_grp"])
    y1 = _mm_resid(mixed, lw["w_out"], x, lw["alpha"], tm=1024)
    xb1, mu1, rs1 = _layer_norm(y1, lw["ln1_g"], lw["ln1_b"], final=False)
    hdn = _gate_up(xb1, lw["w_gate"], lw["w_up"], tm=1024)
    y2 = _mm_resid(hdn, lw["w_down"], (y1, mu1, rs1, lw["ln1_g"], lw["ln1_b"]), lw["alpha"], tm=512)
    if last:
        return _layer_norm(y2, lw["ln2_g"], lw["ln2_b"], final=True), None
    xb2, mu2, rs2 = _layer_norm(y2, lw["ln2_g"], lw["ln2_b"], final=False)
    return (y2, mu2, rs2, lw["ln2_g"], lw["ln2_b"]), xb2


def kernel(x_prompt, x_sample, w_in, g_qn, g_kn, rpb, g_grp, w_out, ln1_g, ln1_b, w_gate, w_up, w_down,
           ln2_g, ln2_b):
    depth = w_in.shape[0]
    layers = []
    for l in range(depth):
        layers.append(dict(
            w_in=_cast_tiled(w_in, l, 512), w_out=_cast_tiled(w_out, l, 512),
            w_gate=_cast_tiled(w_gate, l, 256), w_up=_cast_tiled(w_up, l, 256),
            w_down=_cast_tiled(w_down, l, 256),
            g_qn=g_qn[l].reshape(1, HEAD_DIM), g_kn=g_kn[l].reshape(1, HEAD_DIM),
            tc=_rpb_table(rpb[l]), g_grp=g_grp[l], alpha=(2.0 * depth) ** 0.25,
            ln1_g=ln1_g[l], ln1_b=ln1_b[l], ln2_g=ln2_g[l], ln2_b=ln2_b[l]))
    band_tbl = jnp.asarray(_band_bias_table())
    tabs = _rope_tables(max(x_prompt.shape[1], x_sample.shape[1]))

    def trunk(x3):
        b, s, d = x3.shape
        x = x3.reshape(b * s, d)
        xb = x.astype(BF16)
        for l, lw in enumerate(layers):
            x, xb = _layer(x, xb, b, s, lw, tabs, band_tbl, last=l == depth - 1)
        return x.reshape(b, s, d)

    return (trunk(x_prompt), trunk(x_sample))
```

```python
import functools

import numpy as np
import jax
import jax.numpy as jnp
from jax import lax
from jax.experimental import pallas as pl
from jax.experimental.pallas import tpu as pltpu

F32 = jnp.float32
BF16 = jnp.bfloat16

HEAD_DIM = 128
H_A, H_B, HKV_B, H_C = 8, 16, 4, 8
GRP_B = H_B // HKV_B
DILATED_PATTERNS = ((128, 1), (512, 4), (2048, 16))
ROPE_THETA = 500000.0
ROPE_DIM = HEAD_DIM // 4
AXIAL_THETA = 10000.0
GRID_W = 64
NA_ROWS = 8
NA_COLS = 16
LN_EPS = 1e-5
RMS_EPS = 1e-6
NEG_INF = -1e30
LOG2_E = 1.4426950408889634
Q_SCALE = HEAD_DIM ** -0.5 * LOG2_E

W_A = H_A * HEAD_DIM
W_BQ = H_B * HEAD_DIM
W_BKV = HKV_B * HEAD_DIM
W_C = H_C * HEAD_DIM
OFF_QA, OFF_KA, OFF_VA = 0, W_A, 2 * W_A
OFF_QB = 3 * W_A
OFF_KB = OFF_QB + W_BQ
OFF_VB = OFF_KB + W_BKV
OFF_QC = OFF_VB + W_BKV
OFF_KC = OFF_QC + W_C
OFF_VC = OFF_KC + W_C
IN_WIDTH = OFF_VC + W_C

LANES = 128
VMEM_LIMIT_BYTES = 52 * 1024 * 1024

BAND_REACH = max(w // 2 for w, _ in DILATED_PATTERNS)
BAND_TILE = 256
NA_QROWS = 4
NA_TILE = NA_QROWS * GRID_W
NA_KROWS = 12


def _params(sem):
    return pltpu.CompilerParams(dimension_semantics=sem, vmem_limit_bytes=VMEM_LIMIT_BYTES)


def _rot(h, c, s_up, s_dn, shift):
    return h * c + pltpu.roll(h, LANES - shift, 1) * s_up + pltpu.roll(h, shift, 1) * s_dn


def _inproj_kernel(x_ref, w_ref, ca_ref, ua_ref, da_ref, cb_ref, ub_ref, db_ref, gq_ref, gk_ref,
                   o_ref, *, tn, nchunk):
    j = pl.program_id(1)
    rc = x_ref.shape[0] // nchunk
    heads = tn // HEAD_DIM
    is_rope_a = j < OFF_VA // tn
    is_qb = (j >= OFF_QB // tn) & (j < OFF_KB // tn)
    is_kb = (j >= OFF_KB // tn) & (j < OFF_VB // tn)
    is_plain = jnp.logical_not(is_rope_a | is_qb | is_kb)
    is_q = (j < OFF_KA // tn) | is_qb | ((j >= OFF_QC // tn) & (j < OFF_KC // tn))
    qs = jnp.where(is_q, Q_SCALE, 1.0).astype(F32)

    def run(epilogue):
        for c in range(nchunk):
            rows = slice(c * rc, (c + 1) * rc)
            acc = jnp.dot(x_ref[rows, :], w_ref[...], preferred_element_type=F32)
            for hh in range(heads):
                sl = slice(hh * HEAD_DIM, (hh + 1) * HEAD_DIM)
                o_ref[rows, sl] = (epilogue(acc[:, sl], rows) * qs).astype(o_ref.dtype)

    def rope_a(h, rows):
        return _rot(h, ca_ref[rows, :], ua_ref[rows, :], da_ref[rows, :], ROPE_DIM // 2)

    def norm_rope_b(g_ref):
        def f(h, rows):
            h = h * lax.rsqrt(jnp.mean(h * h, axis=-1, keepdims=True) + RMS_EPS) * g_ref[...]
            return _rot(h, cb_ref[rows, :], ub_ref[rows, :], db_ref[rows, :], HEAD_DIM // 4)
        return f

    pl.when(is_plain)(lambda: run(lambda h, rows: h))
    pl.when(is_rope_a)(lambda: run(rope_a))
    pl.when(is_qb)(lambda: run(norm_rope_b(gq_ref)))
    pl.when(is_kb)(lambda: run(norm_rope_b(gk_ref)))


def _w_spec(w):
    _, k, tn = w.shape
    return pl.BlockSpec((None, k, tn), lambda i, j: (j, 0, 0))


def _in_proj(xb, w, tabs, gq, gk, seq, tm=1024, nchunk=4):
    m, k = xb.shape
    nt, _, tn = w.shape
    assert m % tm == 0 and seq % tm == 0 and tm % nchunk == 0
    for off in (OFF_KA, OFF_VA, OFF_QB, OFF_KB, OFF_VB, OFF_QC, OFF_KC):
        assert off % tn == 0
    nseq = seq // tm
    tab_spec = pl.BlockSpec((tm, LANES), lambda i, j: (i % nseq, 0))
    g_spec = pl.BlockSpec((1, LANES), lambda i, j: (0, 0))
    return pl.pallas_call(
        functools.partial(_inproj_kernel, tn=tn, nchunk=nchunk),
        grid=(m // tm, nt),
        in_specs=[pl.BlockSpec((tm, k), lambda i, j: (i, 0)), _w_spec(w)] + [tab_spec] * 6 + [g_spec, g_spec],
        out_specs=pl.BlockSpec((tm, tn), lambda i, j: (i, j)),
        out_shape=jax.ShapeDtypeStruct((m, nt * tn), BF16),
        compiler_params=_params(("parallel", "arbitrary")),
        name="in_proj",
    )(xb, w, *tabs, gq, gk)


def _cast_kernel(w_ref, o_ref):
    o_ref[...] = w_ref[...].astype(o_ref.dtype)


def _cast_tiled(w3, layer, tn):
    _, k, n = w3.shape
    assert n % tn == 0
    return pl.pallas_call(
        _cast_kernel,
        grid=(n // tn,),
        in_specs=[pl.BlockSpec((None, k, tn), lambda j: (layer, 0, j))],
        out_specs=pl.BlockSpec((None, k, tn), lambda j: (j, 0, 0)),
        out_shape=jax.ShapeDtypeStruct((n // tn, k, tn), BF16),
        compiler_params=_params(("parallel",)),
        name="cast_weight",
    )(w3)


def _rope_tables(seq):
    def cs(pos, theta, d):
        half = d // 2
        inv = theta ** (-jnp.arange(half, dtype=F32) * 2.0 / d)
        ang = pos.astype(F32)[:, None] * inv[None, :]
        return jnp.cos(ang), jnp.sin(ang)

    t = jnp.arange(seq)
    ca, sa = cs(t, ROPE_THETA, ROPE_DIM)
    pad = HEAD_DIM - ROPE_DIM
    z = lambda w_: jnp.zeros((seq, w_), F32)
    tab_ca = jnp.concatenate([ca, ca, jnp.ones((seq, pad), F32)], 1)
    tab_ua = jnp.concatenate([-sa, z(ROPE_DIM // 2 + pad)], 1)
    tab_da = jnp.concatenate([z(ROPE_DIM // 2), sa, z(pad)], 1)
    cr, sr = cs(t // GRID_W, AXIAL_THETA, HEAD_DIM // 2)
    cc, sc = cs(t % GRID_W, AXIAL_THETA, HEAD_DIM // 2)
    q = HEAD_DIM // 4
    tab_cb = jnp.concatenate([cr, cr, cc, cc], 1)
    tab_ub = jnp.concatenate([-sr, z(q), -sc, z(q)], 1)
    tab_db = jnp.concatenate([z(q), sr, z(q), sc], 1)
    return (tab_ca, tab_ua, tab_da, tab_cb, tab_ub, tab_db)


def _mm_resid_kernel(x_ref, w_ref, r_ref, o_ref, *, alpha):
    o_ref[...] = alpha * r_ref[...] + jnp.dot(x_ref[...], w_ref[...], preferred_element_type=F32)


def _ln_apply(y, mu, rs, g, b):
    reps = y.shape[1] // LANES
    mu = jnp.concatenate([mu] * reps, axis=1)
    rs = jnp.concatenate([rs] * reps, axis=1)
    return (y - mu) * rs * g + b


def _mm_resid_ln_kernel(x_ref, w_ref, y_ref, mu_ref, rs_ref, g_ref, b_ref, o_ref, *, alpha):
    resid = _ln_apply(y_ref[...], mu_ref[...], rs_ref[...], g_ref[...], b_ref[...])
    o_ref[...] = alpha * resid + jnp.dot(x_ref[...], w_ref[...], preferred_element_type=F32)


def _mm_resid(xb, w, resid, alpha, tm):
    m, k = xb.shape
    nt, _, tn = w.shape
    assert m % tm == 0
    tile = pl.BlockSpec((tm, tn), lambda i, j: (i, j))
    if isinstance(resid, tuple):
        y, mu, rs, g, b = resid
        stat = pl.BlockSpec((tm, LANES), lambda i, j: (i, 0))
        vec = pl.BlockSpec((1, tn), lambda i, j: (0, j))
        body = _mm_resid_ln_kernel
        extra, extra_specs = (y, mu, rs, g.reshape(1, -1), b.reshape(1, -1)), [tile, stat, stat, vec, vec]
    else:
        body = _mm_resid_kernel
        extra, extra_specs = (resid,), [tile]
    return pl.pallas_call(
        functools.partial(body, alpha=alpha),
        grid=(m // tm, nt),
        in_specs=[pl.BlockSpec((tm, k), lambda i, j: (i, 0)), _w_spec(w)] + extra_specs,
        out_specs=tile,
        out_shape=jax.ShapeDtypeStruct((m, nt * tn), F32),
        compiler_params=_params(("parallel", "arbitrary")),
        name="mm_resid",
    )(xb, w, *extra)


def _gateup_kernel(x_ref, wg_ref, wu_ref, o_ref):
    x = x_ref[...]
    g = jnp.dot(x, wg_ref[...], preferred_element_type=F32)
    u = jnp.dot(x, wu_ref[...], preferred_element_type=F32)
    o_ref[...] = (g / (1.0 + jnp.exp(-g)) * u).astype(o_ref.dtype)


def _gate_up(xb, wg, wu, tm):
    m, k = xb.shape
    nt, _, tn = wg.shape
    assert m % tm == 0 and wu.shape == wg.shape
    return pl.pallas_call(
        _gateup_kernel,
        grid=(m // tm, nt),
        in_specs=[pl.BlockSpec((tm, k), lambda i, j: (i, 0), pipeline_mode=pl.Buffered(1)),
                  _w_spec(wg), _w_spec(wu)],
        out_specs=pl.BlockSpec((tm, tn), lambda i, j: (i, j)),
        out_shape=jax.ShapeDtypeStruct((m, nt * tn), BF16),
        compiler_params=_params(("parallel", "arbitrary")),
        name="gate_up",
    )(xb, wg, wu)


def _ln_stats(y):
    mu = jnp.mean(y, axis=-1, keepdims=True)
    d = y - mu
    var = jnp.mean(d * d, axis=-1, keepdims=True)
    return mu, lax.rsqrt(var + LN_EPS)


def _ln_final_kernel(y_ref, g_ref, b_ref, o_ref):
    y = y_ref[...]
    mu, rs = _ln_stats(y)
    o_ref[...] = (y - mu) * rs * g_ref[...] + b_ref[...]


def _ln_bf16_kernel(y_ref, g_ref, b_ref, ob_ref, mu_ref, rs_ref):
    y = y_ref[...]
    mu, rs = _ln_stats(y)
    ob_ref[...] = ((y - mu) * rs * g_ref[...] + b_ref[...]).astype(ob_ref.dtype)
    mu_ref[...] = jnp.broadcast_to(mu, mu_ref.shape)
    rs_ref[...] = jnp.broadcast_to(rs, rs_ref.shape)


def _layer_norm(y, g, b, final, tm=256):
    m, d = y.shape
    row = pl.BlockSpec((tm, d), lambda i: (i, 0))
    vec = pl.BlockSpec((1, d), lambda i: (0, 0))
    stat = pl.BlockSpec((tm, LANES), lambda i: (i, 0))
    if final:
        body, out_specs, out_shape = _ln_final_kernel, row, jax.ShapeDtypeStruct((m, d), F32)
    else:
        body, out_specs = _ln_bf16_kernel, [row, stat, stat]
        out_shape = [jax.ShapeDtypeStruct((m, d), BF16)] + [jax.ShapeDtypeStruct((m, LANES), F32)] * 2
    return pl.pallas_call(
        body,
        grid=(m // tm,),
        in_specs=[row, vec, vec],
        out_specs=out_specs,
        out_shape=out_shape,
        compiler_params=_params(("parallel",)),
        name="layer_norm",
    )(y, g.reshape(1, d), b.reshape(1, d))


def _mix_kernel(oa_ref, ob_ref, oc_ref, g_ref, o_ref):
    off = 0
    for ref in (oa_ref, ob_ref, oc_ref):
        w = ref.shape[1]
        x = ref[...].astype(F32)
        xn = x * lax.rsqrt(jnp.mean(x * x, axis=-1, keepdims=True) + RMS_EPS)
        o_ref[:, off:off + w] = (xn * g_ref[:, off:off + w]).astype(o_ref.dtype)
        off += w


def _mix(oa, ob, oc, g, tm=512):
    m = oa.shape[0]
    d = oa.shape[1] + ob.shape[1] + oc.shape[1]
    return pl.pallas_call(
        _mix_kernel,
        grid=(m // tm,),
        in_specs=[pl.BlockSpec((tm, oa.shape[1]), lambda i: (i, 0)),
                  pl.BlockSpec((tm, ob.shape[1]), lambda i: (i, 0)),
                  pl.BlockSpec((tm, oc.shape[1]), lambda i: (i, 0)),
                  pl.BlockSpec((1, d), lambda i: (0, 0))],
        out_specs=pl.BlockSpec((tm, d), lambda i: (i, 0)),
        out_shape=jax.ShapeDtypeStruct((m, d), BF16),
        compiler_params=_params(("parallel",)),
        name="group_rms_mix",
    )(oa, ob, oc, g.reshape(1, d))


def _qk(q, k):
    return lax.dot_general(q, k, (((1,), (1,)), ((), ())), preferred_element_type=F32)


def _exp2_rows(s, m):
    return jnp.concatenate([jnp.exp2(s[:, c * LANES:(c + 1) * LANES] - m) for c in range(s.shape[1] // LANES)],
                           axis=1)


def _attn_b_kernel(q_ref, k_ref, v_ref, o_ref, m_ref, l_ref, acc_ref, vx_ref, *, chain):
    kv = pl.program_id(3)
    tq = q_ref.shape[0]
    tk = k_ref.shape[0]

    @pl.when(kv == 0)
    def _():
        m_ref[...] = jnp.full(m_ref.shape, -jnp.inf, F32)
        l_ref[...] = jnp.zeros(l_ref.shape, F32)
        acc_ref[...] = jnp.zeros(acc_ref.shape, F32)
        vx_ref[:, HEAD_DIM:] = jnp.ones((tk, HEAD_DIM), vx_ref.dtype)

    vx_ref[:, :HEAD_DIM] = v_ref[...]
    k = k_ref[...]
    vx = vx_ref[...]
    for r in range(GRP_B):
        for c in range(tq // chain):
            rows = slice(r * tq + c * chain, r * tq + (c + 1) * chain)
            s = _qk(q_ref[c * chain:(c + 1) * chain, r * HEAD_DIM:(r + 1) * HEAD_DIM], k)
            m_prev = m_ref[rows]
            m_new = jnp.maximum(m_prev, jnp.max(s, axis=-1, keepdims=True))
            alpha = jnp.exp2(m_prev - m_new)
            p = _exp2_rows(s, m_new)
            pv = jnp.dot(p.astype(vx.dtype), vx, preferred_element_type=F32)
            acc_ref[rows] = alpha * acc_ref[rows] + pv[:, :HEAD_DIM]
            l_ref[rows] = alpha * l_ref[rows] + pv[:, HEAD_DIM:]
            m_ref[rows] = m_new

    @pl.when(kv == pl.num_programs(3) - 1)
    def _():
        o = acc_ref[...] / l_ref[...]
        for r in range(GRP_B):
            o_ref[:, r * HEAD_DIM:(r + 1) * HEAD_DIM] = o[r * tq:(r + 1) * tq].astype(o_ref.dtype)


def _attn_b(proj3, tq=2048, tk=2048, chain=256):
    b, s, _ = proj3.shape
    qw = GRP_B * HEAD_DIM
    assert s % tq == 0 and s % tk == 0 and tq % chain == 0 and OFF_QB % qw == 0
    stat = pltpu.VMEM((GRP_B * tq, HEAD_DIM), F32)
    return pl.pallas_call(
        functools.partial(_attn_b_kernel, chain=chain),
        grid=(b, HKV_B, s // tq, s // tk),
        in_specs=[pl.BlockSpec((None, tq, qw), lambda bi, g, qi, ki: (bi, qi, OFF_QB // qw + g)),
                  pl.BlockSpec((None, tk, HEAD_DIM), lambda bi, g, qi, ki: (bi, ki, OFF_KB // HEAD_DIM + g)),
                  pl.BlockSpec((None, tk, HEAD_DIM), lambda bi, g, qi, ki: (bi, ki, OFF_VB // HEAD_DIM + g))],
        out_specs=pl.BlockSpec((None, tq, qw), lambda bi, g, qi, ki: (bi, qi, g)),
        out_shape=jax.ShapeDtypeStruct((b, s, W_BQ), BF16),
        scratch_shapes=[stat, stat, stat, pltpu.VMEM((tk, 2 * HEAD_DIM), BF16)],
        compiler_params=_params(("parallel", "parallel", "parallel", "arbitrary")),
        name="attn_axial_gqa",
    )(proj3, proj3, proj3)


def _band_bias_table():
    nt = 2 * BAND_REACH // BAND_TILE + 1
    u = np.arange(nt + 1)[:, None, None] - BAND_REACH // BAND_TILE
    rel = u * BAND_TILE + np.arange(BAND_TILE)[None, None, :] - np.arange(BAND_TILE)[None, :, None]
    mult = np.zeros(rel.shape, np.int64)
    for window, dil in DILATED_PATTERNS:
        mult += (np.abs(rel) <= window // 2) & (rel % dil == 0)
    mult[nt] = 0
    return np.where(mult > 0, np.log2(np.maximum(mult, 1)), NEG_INF).astype(np.float32)


def _attn_a_kernel(q_ref, k_ref, v_ref, tbl_ref, o_ref, vx_ref, *, nblk, wblk, heads, qsub):
    t = BAND_TILE
    r = BAND_REACH // t
    nt = 2 * r + 1
    qi = pl.program_id(2)

    @pl.when(qi == 0)
    def _():
        vx_ref[:, :, HEAD_DIM:] = jnp.ones((qsub * heads, wblk * t, HEAD_DIM), vx_ref.dtype)

    for u in range(qsub):
        qt = qi * qsub + u
        sb = jnp.clip(qt - r, 0, nblk - wblk)
        idx = []
        for c in range(wblk):
            d = sb + c - qt + r
            idx.append(jnp.where((d >= 0) & (d < nt), d, nt))
        qrows = slice(u * t, (u + 1) * t)
        for hh in range(heads):
            ch = u * heads + hh
            hs = slice(hh * HEAD_DIM, (hh + 1) * HEAD_DIM)
            q = q_ref[qrows, hs]
            s = []
            for c in range(wblk):
                rows = pl.ds(pl.multiple_of((sb + c) * t, t), t)
                vx_ref[ch, c * t:(c + 1) * t, :HEAD_DIM] = v_ref[rows, hs]
                s.append(_qk(q, k_ref[rows, hs]) + tbl_ref[idx[c]])
            m = functools.reduce(jnp.maximum, s)
            m = functools.reduce(jnp.maximum, [m[:, c * LANES:(c + 1) * LANES] for c in range(t // LANES)])
            m = jnp.broadcast_to(jnp.max(m, axis=-1, keepdims=True), (t, LANES))
            pv = jnp.zeros((t, 2 * HEAD_DIM), F32)
            for c in range(wblk):
                p = _exp2_rows(s[c], m).astype(vx_ref.dtype)
                pv = pv + jnp.dot(p, vx_ref[ch, c * t:(c + 1) * t, :], preferred_element_type=F32)
            o_ref[qrows, hs] = (pv[:, :HEAD_DIM] / pv[:, HEAD_DIM:]).astype(o_ref.dtype)


def _attn_a(proj3, band_tbl, heads=2, qsub=4):
    b, s, _ = proj3.shape
    t = BAND_TILE
    nblk = s // t
    wblk = min(nblk, 2 * BAND_REACH // t + 1)
    hw = heads * HEAD_DIM
    assert s % (qsub * t) == 0 and H_A % heads == 0 and OFF_KA % hw == 0 and OFF_VA % hw == 0
    kv_spec = lambda off: pl.BlockSpec((None, s, hw), lambda bi, h, qi: (bi, 0, off // hw + h),
                                       pipeline_mode=pl.Buffered(1))
    return pl.pallas_call(
        functools.partial(_attn_a_kernel, nblk=nblk, wblk=wblk, heads=heads, qsub=qsub),
        grid=(b, H_A // heads, nblk // qsub),
        in_specs=[pl.BlockSpec((None, qsub * t, hw), lambda bi, h, qi: (bi, qi, OFF_QA // hw + h)),
                  kv_spec(OFF_KA), kv_spec(OFF_VA),
                  pl.BlockSpec(band_tbl.shape, lambda bi, h, qi: (0, 0, 0), pipeline_mode=pl.Buffered(1))],
        out_specs=pl.BlockSpec((None, qsub * t, hw), lambda bi, h, qi: (bi, qi, h)),
        out_shape=jax.ShapeDtypeStruct((b, s, W_A), BF16),
        scratch_shapes=[pltpu.VMEM((qsub * heads, wblk * t, 2 * HEAD_DIM), BF16)],
        compiler_params=_params(("parallel", "parallel", "arbitrary")),
        name="attn_dilated",
    )(proj3, proj3, proj3, band_tbl)


def _rpb_table_kernel(rpb_ref, o_ref):
    h = pl.program_id(0)
    a = pl.program_id(1) - 1
    n_roff, n_coff = 2 * NA_ROWS - 1, 2 * NA_COLS - 1
    shape = (GRID_W, 2 * GRID_W)
    qc = lax.broadcasted_iota(jnp.int32, shape, 0)
    lane = lax.broadcasted_iota(jnp.int32, shape, 1)
    hi = lane >= GRID_W
    kc = jnp.where(hi, lane - GRID_W, lane)
    coff = jnp.clip(kc - qc, -(NA_COLS - 1), NA_COLS - 1) + (NA_COLS - 1)
    cstart = jnp.clip(qc - NA_COLS // 2, 0, GRID_W - NA_COLS)
    cmask = (kc >= cstart) & (kc < cstart + NA_COLS)
    a_lo = jnp.clip(a, 0, n_roff - 1)
    a_hi = jnp.clip(a + 1, 0, n_roff - 1)
    val = jnp.zeros(shape, F32)
    for bb in range(n_coff):
        r_lo = rpb_ref[(h * n_roff + a_lo) * n_coff + bb]
        r_hi = rpb_ref[(h * n_roff + a_hi) * n_coff + bb]
        val = jnp.where(coff == bb, jnp.where(hi, r_hi, r_lo), val)
    ok_lo = ((a >= 0) & (a < n_roff)).astype(jnp.int32)
    ok_hi = ((a + 1 >= 0) & (a + 1 < n_roff)).astype(jnp.int32)
    ok = jnp.where(hi, ok_hi, ok_lo) > 0
    o_ref[0, 0] = jnp.where(cmask & ok, val * LOG2_E, NEG_INF)


def _rpb_table(rpb):
    h = rpb.shape[0]
    n_slots = 2 * NA_ROWS
    return pl.pallas_call(
        _rpb_table_kernel,
        grid=(h, n_slots),
        in_specs=[pl.BlockSpec(memory_space=pltpu.SMEM)],
        out_specs=pl.BlockSpec((1, 1, GRID_W, 2 * GRID_W), lambda hi, ai: (hi, ai, 0, 0)),
        out_shape=jax.ShapeDtypeStruct((h, n_slots, GRID_W, 2 * GRID_W), F32),
        compiler_params=_params(("arbitrary", "arbitrary")),
        name="rpb_table",
    )(rpb.reshape(-1))


def _attn_c_kernel(q_ref, k_ref, v_ref, tc_ref, o_ref, vx_ref, *, rows, heads, qsub):
    nb = rows // NA_QROWS
    kblk = NA_KROWS // NA_QROWS
    nk = NA_KROWS * GRID_W

    @pl.when(pl.program_id(2) == 0)
    def _():
        vx_ref[:, :, HEAD_DIM:] = jnp.ones((qsub * heads, nk, HEAD_DIM), vx_ref.dtype)

    lane = lax.broadcasted_iota(jnp.int32, (GRID_W, 2 * GRID_W), 1)
    for u in range(qsub):
        b = pl.program_id(2) * qsub + u
        kb = jnp.clip(b - 1, 0, nb - kblk)
        win = pl.ds(pl.multiple_of(kb * NA_TILE, NA_TILE), nk)
        qrows = slice(u * NA_TILE, (u + 1) * NA_TILE)
        sel = []
        for ql in range(NA_QROWS):
            i = b * NA_QROWS + ql
            rstart = jnp.clip(i - NA_ROWS // 2, 0, rows - NA_ROWS)
            for pr in range(NA_KROWS // 2):
                kr = kb * NA_QROWS + 2 * pr
                a = kr - i + (NA_ROWS - 1)
                v_lo = ((kr >= rstart) & (kr < rstart + NA_ROWS)).astype(jnp.int32)
                v_hi = ((kr + 1 >= rstart) & (kr + 1 < rstart + NA_ROWS)).astype(jnp.int32)
                sel.append((jnp.where(lane < GRID_W, v_lo, v_hi) > 0, jnp.clip(a + 1, 0, 2 * NA_ROWS - 1)))
        for hh in range(heads):
            ch = u * heads + hh
            hs = slice(hh * HEAD_DIM, (hh + 1) * HEAD_DIM)
            vx_ref[ch, :, :HEAD_DIM] = v_ref[win, hs]
            strips = []
            for ql in range(NA_QROWS):
                pieces = []
                for pr in range(NA_KROWS // 2):
                    valid, slot = sel[ql * (NA_KROWS // 2) + pr]
                    pieces.append(jnp.where(valid, tc_ref[hh, slot], NEG_INF))
                strips.append(jnp.concatenate(pieces, axis=1))
            s = _qk(q_ref[qrows, hs], k_ref[win, hs]) + jnp.concatenate(strips, axis=0)
            m = functools.reduce(jnp.maximum, [s[:, c * LANES:(c + 1) * LANES] for c in range(nk // LANES)])
            m = jnp.broadcast_to(jnp.max(m, axis=-1, keepdims=True), (NA_TILE, LANES))
            p = _exp2_rows(s, m).astype(vx_ref.dtype)
            pv = jnp.dot(p, vx_ref[ch], preferred_element_type=F32)
            o_ref[qrows, hs] = (pv[:, :HEAD_DIM] / pv[:, HEAD_DIM:]).astype(o_ref.dtype)


def _attn_c(proj3, tc, heads=2, qsub=8):
    b, s, _ = proj3.shape
    rows = s // GRID_W
    hw = heads * HEAD_DIM
    tq = qsub * NA_TILE
    assert s % tq == 0 and rows % NA_QROWS == 0 and rows >= NA_KROWS
    assert H_C % heads == 0 and OFF_QC % hw == 0 and OFF_KC % hw == 0 and OFF_VC % hw == 0
    kv_spec = lambda off: pl.BlockSpec((None, s, hw), lambda bi, h, qi: (bi, 0, off // hw + h),
                                       pipeline_mode=pl.Buffered(1))
    return pl.pallas_call(
        functools.partial(_attn_c_kernel, rows=rows, heads=heads, qsub=qsub),
        grid=(b, H_C // heads, s // tq),
        in_specs=[pl.BlockSpec((None, tq, hw), lambda bi, h, qi: (bi, qi, OFF_QC // hw + h)),
                  kv_spec(OFF_KC), kv_spec(OFF_VC),
                  pl.BlockSpec((heads, 2 * NA_ROWS, GRID_W, 2 * GRID_W), lambda bi, h, qi: (h, 0, 0, 0))],
        out_specs=pl.BlockSpec((None, tq, hw), lambda bi, h, qi: (bi, qi, h)),
        out_shape=jax.ShapeDtypeStruct((b, s, W_C), BF16),
        scratch_shapes=[pltpu.VMEM((qsub * heads, NA_KROWS * GRID_W, 2 * HEAD_DIM), BF16)],
        compiler_params=_params(("parallel", "parallel", "arbitrary")),
        name="attn_neighbourhood",
    )(proj3, proj3, proj3, tc)


def _layer(x, xb, b, s, lw, tabs, band_tbl, last):
    m = xb.shape[0]
    proj = _in_proj(xb, lw["w_in"], tabs, lw["g_qn"], lw["g_kn"], s)
    proj3 = proj.reshape(b, s, IN_WIDTH)
    oa = _attn_a(proj3, band_tbl).reshape(m, W_A)
    ob = _attn_b(proj3).reshape(m, W_BQ)
    oc = _attn_c(proj3, lw["tc"]).reshape(m, W_C)
    mixed = _mix(oa, ob, oc, lw["g_grp"])
    y1 = _mm_resid(mixed, lw["w_out"], x, lw["alpha"], tm=1024)
    xb1, mu1, rs1 = _layer_norm(y1, lw["ln1_g"], lw["ln1_b"], final=False)
    hdn = _gate_up(xb1, lw["w_gate"], lw["w_up"], tm=2048)
    y2 = _mm_resid(hdn, lw["w_down"], (y1, mu1, rs1, lw["ln1_g"], lw["ln1_b"]), lw["alpha"], tm=512)
    if last:
        return _layer_norm(y2, lw["ln2_g"], lw["ln2_b"], final=True), None
    xb2, mu2, rs2 = _layer_norm(y2, lw["ln2_g"], lw["ln2_b"], final=False)
    return (y2, mu2, rs2, lw["ln2_g"], lw["ln2_b"]), xb2


def kernel(x_prompt, x_sample, w_in, g_qn, g_kn, rpb, g_grp, w_out, ln1_g, ln1_b, w_gate, w_up, w_down,
           ln2_g, ln2_b):
    depth = w_in.shape[0]
    layers = []
    for l in range(depth):
        layers.append(dict(
            w_in=_cast_tiled(w_in, l, 512), w_out=_cast_tiled(w_out, l, 512),
            w_gate=_cast_tiled(w_gate, l, 256), w_up=_cast_tiled(w_up, l, 256),
            w_down=_cast_tiled(w_down, l, 256),
            g_qn=g_qn[l].reshape(1, HEAD_DIM), g_kn=g_kn[l].reshape(1, HEAD_DIM),
            tc=_rpb_table(rpb[l]), g_grp=g_grp[l], alpha=(2.0 * depth) ** 0.25,
            ln1_g=ln1_g[l], ln1_b=ln1_b[l], ln2_g=ln2_g[l], ln2_b=ln2_b[l]))
    band_tbl = jnp.asarray(_band_bias_table())
    tabs = _rope_tables(max(x_prompt.shape[1], x_sample.shape[1]))

    def trunk(x3):
        b, s, d = x3.shape
        x = x3.reshape(b * s, d)
        xb = x.astype(BF16)
        for l, lw in enumerate(layers):
            x, xb = _layer(x, xb, b, s, lw, tabs, band_tbl, last=l == depth - 1)
        return x.reshape(b, s, d)

    return (trunk(x_prompt), trunk(x_sample))
```

```python
import functools

import numpy as np
import jax
import jax.numpy as jnp
from jax import lax
from jax.experimental import pallas as pl
from jax.experimental.pallas import tpu as pltpu

F32 = jnp.float32
BF16 = jnp.bfloat16

HEAD_DIM = 128
H_A, H_B, HKV_B, H_C = 8, 16, 4, 8
GRP_B = H_B // HKV_B
DILATED_PATTERNS = ((128, 1), (512, 4), (2048, 16))
ROPE_THETA = 500000.0
ROPE_DIM = HEAD_DIM // 4
AXIAL_THETA = 10000.0
GRID_W = 64
NA_ROWS = 8
NA_COLS = 16
LN_EPS = 1e-5
RMS_EPS = 1e-6
NEG_INF = -1e30
LOG2_E = 1.4426950408889634
Q_SCALE = HEAD_DIM ** -0.5 * LOG2_E

W_A = H_A * HEAD_DIM
W_BQ = H_B * HEAD_DIM
W_BKV = HKV_B * HEAD_DIM
W_C = H_C * HEAD_DIM
OFF_QA, OFF_KA, OFF_VA = 0, W_A, 2 * W_A
OFF_QB = 3 * W_A
OFF_KB = OFF_QB + W_BQ
OFF_VB = OFF_KB + W_BKV
OFF_QC = OFF_VB + W_BKV
OFF_KC = OFF_QC + W_C
OFF_VC = OFF_KC + W_C
IN_WIDTH = OFF_VC + W_C

LANES = 128
VMEM_LIMIT_BYTES = 52 * 1024 * 1024

BAND_REACH = max(w // 2 for w, _ in DILATED_PATTERNS)
BAND_TILE = 256
NA_QROWS = 4
NA_TILE = NA_QROWS * GRID_W
NA_KROWS = 12


def _params(sem):
    return pltpu.CompilerParams(dimension_semantics=sem, vmem_limit_bytes=VMEM_LIMIT_BYTES)


def _rot(h, c, s_up, s_dn, shift):
    return h * c + pltpu.roll(h, LANES - shift, 1) * s_up + pltpu.roll(h, shift, 1) * s_dn


def _inproj_kernel(x_ref, w_ref, ca_ref, ua_ref, da_ref, cb_ref, ub_ref, db_ref, gq_ref, gk_ref,
                   o_ref, *, tn, nchunk):
    j = pl.program_id(1)
    rc = x_ref.shape[0] // nchunk
    heads = tn // HEAD_DIM
    is_rope_a = j < OFF_VA // tn
    is_qb = (j >= OFF_QB // tn) & (j < OFF_KB // tn)
    is_kb = (j >= OFF_KB // tn) & (j < OFF_VB // tn)
    is_plain = jnp.logical_not(is_rope_a | is_qb | is_kb)
    is_q = (j < OFF_KA // tn) | is_qb | ((j >= OFF_QC // tn) & (j < OFF_KC // tn))
    qs = jnp.where(is_q, Q_SCALE, 1.0).astype(F32)

    def run(epilogue):
        for c in range(nchunk):
            rows = slice(c * rc, (c + 1) * rc)
            acc = jnp.dot(x_ref[rows, :], w_ref[...], preferred_element_type=F32)
            for hh in range(heads):
                sl = slice(hh * HEAD_DIM, (hh + 1) * HEAD_DIM)
                o_ref[rows, sl] = (epilogue(acc[:, sl], rows) * qs).astype(o_ref.dtype)

    def rope_a(h, rows):
        return _rot(h, ca_ref[rows, :], ua_ref[rows, :], da_ref[rows, :], ROPE_DIM // 2)

    def norm_rope_b(g_ref):
        def f(h, rows):
            h = h * lax.rsqrt(jnp.mean(h * h, axis=-1, keepdims=True) + RMS_EPS) * g_ref[...]
            return _rot(h, cb_ref[rows, :], ub_ref[rows, :], db_ref[rows, :], HEAD_DIM // 4)
        return f

    pl.when(is_plain)(lambda: run(lambda h, rows: h))
    pl.when(is_rope_a)(lambda: run(rope_a))
    pl.when(is_qb)(lambda: run(norm_rope_b(gq_ref)))
    pl.when(is_kb)(lambda: run(norm_rope_b(gk_ref)))


def _w_spec(w):
    _, k, tn = w.shape
    return pl.BlockSpec((None, k, tn), lambda i, j: (j, 0, 0))


def _in_proj(xb, w, tabs, gq, gk, seq, tm=1024, nchunk=4):
    m, k = xb.shape
    nt, _, tn = w.shape
    assert m % tm == 0 and seq % tm == 0 and tm % nchunk == 0
    for off in (OFF_KA, OFF_VA, OFF_QB, OFF_KB, OFF_VB, OFF_QC, OFF_KC):
        assert off % tn == 0
    nseq = seq // tm
    tab_spec = pl.BlockSpec((tm, LANES), lambda i, j: (i % nseq, 0))
    g_spec = pl.BlockSpec((1, LANES), lambda i, j: (0, 0))
    return pl.pallas_call(
        functools.partial(_inproj_kernel, tn=tn, nchunk=nchunk),
        grid=(m // tm, nt),
        in_specs=[pl.BlockSpec((tm, k), lambda i, j: (i, 0)), _w_spec(w)] + [tab_spec] * 6 + [g_spec, g_spec],
        out_specs=pl.BlockSpec((tm, tn), lambda i, j: (i, j)),
        out_shape=jax.ShapeDtypeStruct((m, nt * tn), BF16),
        compiler_params=_params(("parallel", "arbitrary")),
        name="in_proj",
    )(xb, w, *tabs, gq, gk)


def _cast_kernel(w_ref, o_ref):
    o_ref[...] = w_ref[...].astype(o_ref.dtype)


def _cast_tiled(w3, layer, tn):
    _, k, n = w3.shape
    assert n % tn == 0
    return pl.pallas_call(
        _cast_kernel,
        grid=(n // tn,),
        in_specs=[pl.BlockSpec((None, k, tn), lambda j: (layer, 0, j))],
        out_specs=pl.BlockSpec((None, k, tn), lambda j: (j, 0, 0)),
        out_shape=jax.ShapeDtypeStruct((n // tn, k, tn), BF16),
        compiler_params=_params(("parallel",)),
        name="cast_weight",
    )(w3)


def _rope_tables(seq):
    def cs(pos, theta, d):
        half = d // 2
        inv = theta ** (-jnp.arange(half, dtype=F32) * 2.0 / d)
        ang = pos.astype(F32)[:, None] * inv[None, :]
        return jnp.cos(ang), jnp.sin(ang)

    t = jnp.arange(seq)
    ca, sa = cs(t, ROPE_THETA, ROPE_DIM)
    pad = HEAD_DIM - ROPE_DIM
    z = lambda w_: jnp.zeros((seq, w_), F32)
    tab_ca = jnp.concatenate([ca, ca, jnp.ones((seq, pad), F32)], 1)
    tab_ua = jnp.concatenate([-sa, z(ROPE_DIM // 2 + pad)], 1)
    tab_da = jnp.concatenate([z(ROPE_DIM // 2), sa, z(pad)], 1)
    cr, sr = cs(t // GRID_W, AXIAL_THETA, HEAD_DIM // 2)
    cc, sc = cs(t % GRID_W, AXIAL_THETA, HEAD_DIM // 2)
    q = HEAD_DIM // 4
    tab_cb = jnp.concatenate([cr, cr, cc, cc], 1)
    tab_ub = jnp.concatenate([-sr, z(q), -sc, z(q)], 1)
    tab_db = jnp.concatenate([z(q), sr, z(q), sc], 1)
    return (tab_ca, tab_ua, tab_da, tab_cb, tab_ub, tab_db)


def _mm_resid_kernel(x_ref, w_ref, r_ref, o_ref, *, alpha):
    o_ref[...] = alpha * r_ref[...] + jnp.dot(x_ref[...], w_ref[...], preferred_element_type=F32)


def _ln_apply(y, mu, rs, g, b):
    reps = y.shape[1] // LANES
    mu = jnp.concatenate([mu] * reps, axis=1)
    rs = jnp.concatenate([rs] * reps, axis=1)
    return (y - mu) * rs * g + b


def _mm_resid_ln_kernel(x_ref, w_ref, y_ref, mu_ref, rs_ref, g_ref, b_ref, o_ref, *, alpha):
    resid = _ln_apply(y_ref[...], mu_ref[...], rs_ref[...], g_ref[...], b_ref[...])
    o_ref[...] = alpha * resid + jnp.dot(x_ref[...], w_ref[...], preferred_element_type=F32)


def _mm_resid(xb, w, resid, alpha, tm):
    m, k = xb.shape
    nt, _, tn = w.shape
    assert m % tm == 0
    tile = pl.BlockSpec((tm, tn), lambda i, j: (i, j))
    if isinstance(resid, tuple):
        y, mu, rs, g, b = resid
        stat = pl.BlockSpec((tm, LANES), lambda i, j: (i, 0))
        vec = pl.BlockSpec((1, tn), lambda i, j: (0, j))
        body = _mm_resid_ln_kernel
        extra, extra_specs = (y, mu, rs, g.reshape(1, -1), b.reshape(1, -1)), [tile, stat, stat, vec, vec]
    else:
        body = _mm_resid_kernel
        extra, extra_specs = (resid,), [tile]
    return pl.pallas_call(
        functools.partial(body, alpha=alpha),
        grid=(m // tm, nt),
        in_specs=[pl.BlockSpec((tm, k), lambda i, j: (i, 0)), _w_spec(w)] + extra_specs,
        out_specs=tile,
        out_shape=jax.ShapeDtypeStruct((m, nt * tn), F32),
        compiler_params=_params(("parallel", "arbitrary")),
        name="mm_resid",
    )(xb, w, *extra)


def _gateup_kernel(x_ref, wg_ref, wu_ref, o_ref):
    x = x_ref[...]
    g = jnp.dot(x, wg_ref[...], preferred_element_type=F32)
    u = jnp.dot(x, wu_ref[...], preferred_element_type=F32)
    o_ref[...] = (g / (1.0 + jnp.exp(-g)) * u).astype(o_ref.dtype)


def _gate_up(xb, wg, wu, tm):
    m, k = xb.shape
    nt, _, tn = wg.shape
    assert m % tm == 0 and wu.shape == wg.shape
    return pl.pallas_call(
        _gateup_kernel,
        grid=(m // tm, nt),
        in_specs=[pl.BlockSpec((tm, k), lambda i, j: (i, 0)), _w_spec(wg), _w_spec(wu)],
        out_specs=pl.BlockSpec((tm, tn), lambda i, j: (i, j)),
        out_shape=jax.ShapeDtypeStruct((m, nt * tn), BF16),
        compiler_params=_params(("parallel", "arbitrary")),
        name="gate_up",
    )(xb, wg, wu)


def _ln_stats(y):
    mu = jnp.mean(y, axis=-1, keepdims=True)
    d = y - mu
    var = jnp.mean(d * d, axis=-1, keepdims=True)
    return mu, lax.rsqrt(var + LN_EPS)


def _ln_final_kernel(y_ref, g_ref, b_ref, o_ref):
    y = y_ref[...]
    mu, rs = _ln_stats(y)
    o_ref[...] = (y - mu) * rs * g_ref[...] + b_ref[...]


def _ln_bf16_kernel(y_ref, g_ref, b_ref, ob_ref, mu_ref, rs_ref):
    y = y_ref[...]
    mu, rs = _ln_stats(y)
    ob_ref[...] = ((y - mu) * rs * g_ref[...] + b_ref[...]).astype(ob_ref.dtype)
    mu_ref[...] = jnp.broadcast_to(mu, mu_ref.shape)
    rs_ref[...] = jnp.broadcast_to(rs, rs_ref.shape)


def _layer_norm(y, g, b, final, tm=256):
    m, d = y.shape
    row = pl.BlockSpec((tm, d), lambda i: (i, 0))
    vec = pl.BlockSpec((1, d), lambda i: (0, 0))
    stat = pl.BlockSpec((tm, LANES), lambda i: (i, 0))
    if final:
        body, out_specs, out_shape = _ln_final_kernel, row, jax.ShapeDtypeStruct((m, d), F32)
    else:
        body, out_specs = _ln_bf16_kernel, [row, stat, stat]
        out_shape = [jax.ShapeDtypeStruct((m, d), BF16)] + [jax.ShapeDtypeStruct((m, LANES), F32)] * 2
    return pl.pallas_call(
        body,
        grid=(m // tm,),
        in_specs=[row, vec, vec],
        out_specs=out_specs,
        out_shape=out_shape,
        compiler_params=_params(("parallel",)),
        name="layer_norm",
    )(y, g.reshape(1, d), b.reshape(1, d))


def _mix_kernel(oa_ref, ob_ref, oc_ref, g_ref, o_ref):
    off = 0
    for ref in (oa_ref, ob_ref, oc_ref):
        w = ref.shape[1]
        x = ref[...].astype(F32)
        xn = x * lax.rsqrt(jnp.mean(x * x, axis=-1, keepdims=True) + RMS_EPS)
        o_ref[:, off:off + w] = (xn * g_ref[:, off:off + w]).astype(o_ref.dtype)
        off += w


def _mix(oa, ob, oc, g, tm=512):
    m = oa.shape[0]
    d = oa.shape[1] + ob.shape[1] + oc.shape[1]
    return pl.pallas_call(
        _mix_kernel,
        grid=(m // tm,),
        in_specs=[pl.BlockSpec((tm, oa.shape[1]), lambda i: (i, 0)),
                  pl.BlockSpec((tm, ob.shape[1]), lambda i: (i, 0)),
                  pl.BlockSpec((tm, oc.shape[1]), lambda i: (i, 0)),
                  pl.BlockSpec((1, d), lambda i: (0, 0))],
        out_specs=pl.BlockSpec((tm, d), lambda i: (i, 0)),
        out_shape=jax.ShapeDtypeStruct((m, d), BF16),
        compiler_params=_params(("parallel",)),
        name="group_rms_mix",
    )(oa, ob, oc, g.reshape(1, d))


def _qk(q, k):
    return lax.dot_general(q, k, (((1,), (1,)), ((), ())), preferred_element_type=F32)


def _exp2_rows(s, m):
    return jnp.concatenate([jnp.exp2(s[:, c * LANES:(c + 1) * LANES] - m) for c in range(s.shape[1] // LANES)],
                           axis=1)


ONES_ROWS = 16
QK_AHEAD = 4


def _attn_b_kernel(q_ref, k_ref, v_ref, o_ref, m_ref, acc_ref, vxt_ref, *, chain):
    kv = pl.program_id(3)
    tq = q_ref.shape[0]
    tk = k_ref.shape[0]

    @pl.when(kv == 0)
    def _():
        m_ref[...] = jnp.full(m_ref.shape, -jnp.inf, F32)
        acc_ref[...] = jnp.zeros(acc_ref.shape, F32)
        vxt_ref[HEAD_DIM:, :] = jnp.ones((ONES_ROWS, tk), vxt_ref.dtype)

    vxt_ref[:HEAD_DIM, :] = v_ref[...].astype(F32).T.astype(vxt_ref.dtype)
    k = k_ref[...]
    vxt = vxt_ref[...]
    nc = tq // chain
    chains = [(r, c) for r in range(GRP_B) for c in range(nc)]

    def scores(r, c):
        return _qk(k, q_ref[c * chain:(c + 1) * chain, r * HEAD_DIM:(r + 1) * HEAD_DIM])

    pending = [scores(*chains[i]) for i in range(min(QK_AHEAD, len(chains)))]
    for ch in range(len(chains)):
        st = pending.pop(0)
        if ch + QK_AHEAD < len(chains):
            pending.append(scores(*chains[ch + QK_AHEAD]))
        m_prev = m_ref[ch * 8:(ch + 1) * 8, :]
        m_new = jnp.maximum(m_prev, jnp.max(st, axis=0, keepdims=True))
        alpha = jnp.exp2(m_prev - m_new)[0:1, :]
        p = jnp.exp2(st - m_new[0:1, :]).astype(vxt.dtype)
        acc_ref[ch] = alpha * acc_ref[ch] + jnp.dot(vxt, p, preferred_element_type=F32)
        m_ref[ch * 8:(ch + 1) * 8, :] = m_new

    @pl.when(kv == pl.num_programs(3) - 1)
    def _():
        for ch, (r, c) in enumerate(chains):
            a = acc_ref[ch]
            o = (a[:HEAD_DIM, :] / a[HEAD_DIM:HEAD_DIM + 1, :]).T
            o_ref[c * chain:(c + 1) * chain, r * HEAD_DIM:(r + 1) * HEAD_DIM] = o.astype(o_ref.dtype)


def _attn_b(proj3, tq=2048, tk=2048, chain=256):
    b, s, _ = proj3.shape
    qw = GRP_B * HEAD_DIM
    assert s % tq == 0 and s % tk == 0 and tq % chain == 0 and OFF_QB % qw == 0
    nchain = GRP_B * tq // chain
    return pl.pallas_call(
        functools.partial(_attn_b_kernel, chain=chain),
        grid=(b, HKV_B, s // tq, s // tk),
        in_specs=[pl.BlockSpec((None, tq, qw), lambda bi, g, qi, ki: (bi, qi, OFF_QB // qw + g)),
                  pl.BlockSpec((None, tk, HEAD_DIM), lambda bi, g, qi, ki: (bi, ki, OFF_KB // HEAD_DIM + g)),
                  pl.BlockSpec((None, tk, HEAD_DIM), lambda bi, g, qi, ki: (bi, ki, OFF_VB // HEAD_DIM + g))],
        out_specs=pl.BlockSpec((None, tq, qw), lambda bi, g, qi, ki: (bi, qi, g)),
        out_shape=jax.ShapeDtypeStruct((b, s, W_BQ), BF16),
        scratch_shapes=[pltpu.VMEM((nchain * 8, chain), F32),
                        pltpu.VMEM((nchain, HEAD_DIM + ONES_ROWS, chain), F32),
                        pltpu.VMEM((HEAD_DIM + ONES_ROWS, tk), BF16)],
        compiler_params=_params(("parallel", "parallel", "parallel", "arbitrary")),
        name="attn_axial_gqa",
    )(proj3, proj3, proj3)


def _band_bias_table():
    nt = 2 * BAND_REACH // BAND_TILE + 1
    u = np.arange(nt + 1)[:, None, None] - BAND_REACH // BAND_TILE
    rel = u * BAND_TILE + np.arange(BAND_TILE)[None, None, :] - np.arange(BAND_TILE)[None, :, None]
    mult = np.zeros(rel.shape, np.int64)
    for window, dil in DILATED_PATTERNS:
        mult += (np.abs(rel) <= window // 2) & (rel % dil == 0)
    mult[nt] = 0
    return np.where(mult > 0, np.log2(np.maximum(mult, 1)), NEG_INF).astype(np.float32)


def _attn_a_kernel(q_ref, k_ref, v_ref, tbl_ref, o_ref, vx_ref, *, nblk, wblk, heads, qsub):
    t = BAND_TILE
    r = BAND_REACH // t
    nt = 2 * r + 1
    qi = pl.program_id(2)

    @pl.when(qi == 0)
    def _():
        vx_ref[:, :, HEAD_DIM:] = jnp.ones((qsub * heads, wblk * t, HEAD_DIM), vx_ref.dtype)

    for u in range(qsub):
        qt = qi * qsub + u
        sb = jnp.clip(qt - r, 0, nblk - wblk)
        idx = []
        for c in range(wblk):
            d = sb + c - qt + r
            idx.append(jnp.where((d >= 0) & (d < nt), d, nt))
        qrows = slice(u * t, (u + 1) * t)
        for hh in range(heads):
            ch = u * heads + hh
            hs = slice(hh * HEAD_DIM, (hh + 1) * HEAD_DIM)
            q = q_ref[qrows, hs]
            s = []
            for c in range(wblk):
                rows = pl.ds(pl.multiple_of((sb + c) * t, t), t)
                vx_ref[ch, c * t:(c + 1) * t, :HEAD_DIM] = v_ref[rows, hs]
                s.append(_qk(q, k_ref[rows, hs]) + tbl_ref[idx[c]])
            m = functools.reduce(jnp.maximum, s)
            m = functools.reduce(jnp.maximum, [m[:, c * LANES:(c + 1) * LANES] for c in range(t // LANES)])
            m = jnp.broadcast_to(jnp.max(m, axis=-1, keepdims=True), (t, LANES))
            pv = jnp.zeros((t, 2 * HEAD_DIM), F32)
            for c in range(wblk):
                p = _exp2_rows(s[c], m).astype(vx_ref.dtype)
                pv = pv + jnp.dot(p, vx_ref[ch, c * t:(c + 1) * t, :], preferred_element_type=F32)
            o_ref[qrows, hs] = (pv[:, :HEAD_DIM] / pv[:, HEAD_DIM:]).astype(o_ref.dtype)


def _attn_a(proj3, band_tbl, heads=2, qsub=4):
    b, s, _ = proj3.shape
    t = BAND_TILE
    nblk = s // t
    wblk = min(nblk, 2 * BAND_REACH // t + 1)
    hw = heads * HEAD_DIM
    assert s % (qsub * t) == 0 and H_A % heads == 0 and OFF_KA % hw == 0 and OFF_VA % hw == 0
    kv_spec = lambda off: pl.BlockSpec((None, s, hw), lambda bi, h, qi: (bi, 0, off // hw + h),
                                       pipeline_mode=pl.Buffered(1))
    return pl.pallas_call(
        functools.partial(_attn_a_kernel, nblk=nblk, wblk=wblk, heads=heads, qsub=qsub),
        grid=(b, H_A // heads, nblk // qsub),
        in_specs=[pl.BlockSpec((None, qsub * t, hw), lambda bi, h, qi: (bi, qi, OFF_QA // hw + h)),
                  kv_spec(OFF_KA), kv_spec(OFF_VA),
                  pl.BlockSpec(band_tbl.shape, lambda bi, h, qi: (0, 0, 0), pipeline_mode=pl.Buffered(1))],
        out_specs=pl.BlockSpec((None, qsub * t, hw), lambda bi, h, qi: (bi, qi, h)),
        out_shape=jax.ShapeDtypeStruct((b, s, W_A), BF16),
        scratch_shapes=[pltpu.VMEM((qsub * heads, wblk * t, 2 * HEAD_DIM), BF16)],
        compiler_params=_params(("parallel", "parallel", "arbitrary")),
        name="attn_dilated",
    )(proj3, proj3, proj3, band_tbl)


def _rpb_table_kernel(rpb_ref, o_ref):
    h = pl.program_id(0)
    a = pl.program_id(1) - 1
    n_roff, n_coff = 2 * NA_ROWS - 1, 2 * NA_COLS - 1
    shape = (GRID_W, 2 * GRID_W)
    qc = lax.broadcasted_iota(jnp.int32, shape, 0)
    lane = lax.broadcasted_iota(jnp.int32, shape, 1)
    hi = lane >= GRID_W
    kc = jnp.where(hi, lane - GRID_W, lane)
    coff = jnp.clip(kc - qc, -(NA_COLS - 1), NA_COLS - 1) + (NA_COLS - 1)
    cstart = jnp.clip(qc - NA_COLS // 2, 0, GRID_W - NA_COLS)
    cmask = (kc >= cstart) & (kc < cstart + NA_COLS)
    a_lo = jnp.clip(a, 0, n_roff - 1)
    a_hi = jnp.clip(a + 1, 0, n_roff - 1)
    val = jnp.zeros(shape, F32)
    for bb in range(n_coff):
        r_lo = rpb_ref[(h * n_roff + a_lo) * n_coff + bb]
        r_hi = rpb_ref[(h * n_roff + a_hi) * n_coff + bb]
        val = jnp.where(coff == bb, jnp.where(hi, r_hi, r_lo), val)
    ok_lo = ((a >= 0) & (a < n_roff)).astype(jnp.int32)
    ok_hi = ((a + 1 >= 0) & (a + 1 < n_roff)).astype(jnp.int32)
    ok = jnp.where(hi, ok_hi, ok_lo) > 0
    o_ref[0, 0] = jnp.where(cmask & ok, val * LOG2_E, NEG_INF)


def _rpb_table(rpb):
    h = rpb.shape[0]
    n_slots = 2 * NA_ROWS
    return pl.pallas_call(
        _rpb_table_kernel,
        grid=(h, n_slots),
        in_specs=[pl.BlockSpec(memory_space=pltpu.SMEM)],
        out_specs=pl.BlockSpec((1, 1, GRID_W, 2 * GRID_W), lambda hi, ai: (hi, ai, 0, 0)),
        out_shape=jax.ShapeDtypeStruct((h, n_slots, GRID_W, 2 * GRID_W), F32),
        compiler_params=_params(("arbitrary", "arbitrary")),
        name="rpb_table",
    )(rpb.reshape(-1))


def _attn_c_kernel(q_ref, k_ref, v_ref, tc_ref, o_ref, vx_ref, *, rows, heads, qsub):
    nb = rows // NA_QROWS
    kblk = NA_KROWS // NA_QROWS
    nk = NA_KROWS * GRID_W

    @pl.when(pl.program_id(2) == 0)
    def _():
        vx_ref[:, :, HEAD_DIM:] = jnp.ones((qsub * heads, nk, HEAD_DIM), vx_ref.dtype)

    lane = lax.broadcasted_iota(jnp.int32, (GRID_W, 2 * GRID_W), 1)
    for u in range(qsub):
        b = pl.program_id(2) * qsub + u
        kb = jnp.clip(b - 1, 0, nb - kblk)
        win = pl.ds(pl.multiple_of(kb * NA_TILE, NA_TILE), nk)
        qrows = slice(u * NA_TILE, (u + 1) * NA_TILE)
        sel = []
        for ql in range(NA_QROWS):
            i = b * NA_QROWS + ql
            rstart = jnp.clip(i - NA_ROWS // 2, 0, rows - NA_ROWS)
            for pr in range(NA_KROWS // 2):
                kr = kb * NA_QROWS + 2 * pr
                a = kr - i + (NA_ROWS - 1)
                v_lo = ((kr >= rstart) & (kr < rstart + NA_ROWS)).astype(jnp.int32)
                v_hi = ((kr + 1 >= rstart) & (kr + 1 < rstart + NA_ROWS)).astype(jnp.int32)
                sel.append((jnp.where(lane < GRID_W, v_lo, v_hi) > 0, jnp.clip(a + 1, 0, 2 * NA_ROWS - 1)))
        for hh in range(heads):
            ch = u * heads + hh
            hs = slice(hh * HEAD_DIM, (hh + 1) * HEAD_DIM)
            vx_ref[ch, :, :HEAD_DIM] = v_ref[win, hs]
            strips = []
            for ql in range(NA_QROWS):
                pieces = []
                for pr in range(NA_KROWS // 2):
                    valid, slot = sel[ql * (NA_KROWS // 2) + pr]
                    pieces.append(jnp.where(valid, tc_ref[hh, slot], NEG_INF))
                strips.append(jnp.concatenate(pieces, axis=1))
            s = _qk(q_ref[qrows, hs], k_ref[win, hs]) + jnp.concatenate(strips, axis=0)
            m = functools.reduce(jnp.maximum, [s[:, c * LANES:(c + 1) * LANES] for c in range(nk // LANES)])
            m = jnp.broadcast_to(jnp.max(m, axis=-1, keepdims=True), (NA_TILE, LANES))
            p = _exp2_rows(s, m).astype(vx_ref.dtype)
            pv = jnp.dot(p, vx_ref[ch], preferred_element_type=F32)
            o_ref[qrows, hs] = (pv[:, :HEAD_DIM] / pv[:, HEAD_DIM:]).astype(o_ref.dtype)


def _attn_c(proj3, tc, heads=2, qsub=8):
    b, s, _ = proj3.shape
    rows = s // GRID_W
    hw = heads * HEAD_DIM
    tq = qsub * NA_TILE
    assert s % tq == 0 and rows % NA_QROWS == 0 and rows >= NA_KROWS
    assert H_C % heads == 0 and OFF_QC % hw == 0 and OFF_KC % hw == 0 and OFF_VC % hw == 0
    kv_spec = lambda off: pl.BlockSpec((None, s, hw), lambda bi, h, qi: (bi, 0, off // hw + h),
                                       pipeline_mode=pl.Buffered(1))
    return pl.pallas_call(
        functools.partial(_attn_c_kernel, rows=rows, heads=heads, qsub=qsub),
        grid=(b, H_C // heads, s // tq),
        in_specs=[pl.BlockSpec((None, tq, hw), lambda bi, h, qi: (bi, qi, OFF_QC // hw + h)),
                  kv_spec(OFF_KC), kv_spec(OFF_VC),
                  pl.BlockSpec((heads, 2 * NA_ROWS, GRID_W, 2 * GRID_W), lambda bi, h, qi: (h, 0, 0, 0))],
        out_specs=pl.BlockSpec((None, tq, hw), lambda bi, h, qi: (bi, qi, h)),
        out_shape=jax.ShapeDtypeStruct((b, s, W_C), BF16),
        scratch_shapes=[pltpu.VMEM((qsub * heads, NA_KROWS * GRID_W, 2 * HEAD_DIM), BF16)],
        compiler_params=_params(("parallel", "parallel", "arbitrary")),
        name="attn_neighbourhood",
    )(proj3, proj3, proj3, tc)


def _layer(x, xb, b, s, lw, tabs, band_tbl, last):
    m = xb.shape[0]
    proj = _in_proj(xb, lw["w_in"], tabs, lw["g_qn"], lw["g_kn"], s)
    proj3 = proj.reshape(b, s, IN_WIDTH)
    oa = _attn_a(proj3, band_tbl).reshape(m, W_A)
    ob = _attn_b(proj3).reshape(m, W_BQ)
    oc = _attn_c(proj3, lw["tc"]).reshape(m, W_C)
    mixed = _mix(oa, ob, oc, lw["g_grp"])
    y1 = _mm_resid(mixed, lw["w_out"], x, lw["alpha"], tm=1024)
    xb1, mu1, rs1 = _layer_norm(y1, lw["ln1_g"], lw["ln1_b"], final=False)
    hdn = _gate_up(xb1, lw["w_gate"], lw["w_up"], tm=1024)
    y2 = _mm_resid(hdn, lw["w_down"], (y1, mu1, rs1, lw["ln1_g"], lw["ln1_b"]), lw["alpha"], tm=512)
    if last:
        return _layer_norm(y2, lw["ln2_g"], lw["ln2_b"], final=True), None
    xb2, mu2, rs2 = _layer_norm(y2, lw["ln2_g"], lw["ln2_b"], final=False)
    return (y2, mu2, rs2, lw["ln2_g"], lw["ln2_b"]), xb2


def kernel(x_prompt, x_sample, w_in, g_qn, g_kn, rpb, g_grp, w_out, ln1_g, ln1_b, w_gate, w_up, w_down,
           ln2_g, ln2_b):
    depth = w_in.shape[0]
    layers = []
    for l in range(depth):
        layers.append(dict(
            w_in=_cast_tiled(w_in, l, 512), w_out=_cast_tiled(w_out, l, 512),
            w_gate=_cast_tiled(w_gate, l, 256), w_up=_cast_tiled(w_up, l, 256),
            w_down=_cast_tiled(w_down, l, 256),
            g_qn=g_qn[l].reshape(1, HEAD_DIM), g_kn=g_kn[l].reshape(1, HEAD_DIM),
            tc=_rpb_table(rpb[l]), g_grp=g_grp[l], alpha=(2.0 * depth) ** 0.25,
            ln1_g=ln1_g[l], ln1_b=ln1_b[l], ln2_g=ln2_g[l], ln2_b=ln2_b[l]))
    band_tbl = jnp.asarray(_band_bias_table())
    tabs = _rope_tables(max(x_prompt.shape[1], x_sample.shape[1]))

    def trunk(x3):
        b, s, d = x3.shape
        x = x3.reshape(b * s, d)
        xb = x.astype(BF16)
        for l, lw in enumerate(layers):
            x, xb = _layer(x, xb, b, s, lw, tabs, band_tbl, last=l == depth - 1)
        return x.reshape(b, s, d)

    return (trunk(x_prompt), trunk(x_sample))
```

```python
import functools

import numpy as np
import jax
import jax.numpy as jnp
from jax import lax
from jax.experimental import pallas as pl
from jax.experimental.pallas import tpu as pltpu

F32 = jnp.float32
BF16 = jnp.bfloat16

HEAD_DIM = 128
H_A, H_B, HKV_B, H_C = 8, 16, 4, 8
GRP_B = H_B // HKV_B
DILATED_PATTERNS = ((128, 1), (512, 4), (2048, 16))
ROPE_THETA = 500000.0
ROPE_DIM = HEAD_DIM // 4
AXIAL_THETA = 10000.0
GRID_W = 64
NA_ROWS = 8
NA_COLS = 16
LN_EPS = 1e-5
RMS_EPS = 1e-6
NEG_INF = -1e30
LOG2_E = 1.4426950408889634
Q_SCALE = HEAD_DIM ** -0.5 * LOG2_E

W_A = H_A * HEAD_DIM
W_BQ = H_B * HEAD_DIM
W_BKV = HKV_B * HEAD_DIM
W_C = H_C * HEAD_DIM
OFF_QA, OFF_KA, OFF_VA = 0, W_A, 2 * W_A
OFF_QB = 3 * W_A
OFF_KB = OFF_QB + W_BQ
OFF_VB = OFF_KB + W_BKV
OFF_QC = OFF_VB + W_BKV
OFF_KC = OFF_QC + W_C
OFF_VC = OFF_KC + W_C
IN_WIDTH = OFF_VC + W_C

LANES = 128
VMEM_LIMIT_BYTES = 52 * 1024 * 1024

BAND_REACH = max(w // 2 for w, _ in DILATED_PATTERNS)
BAND_TILE = 256
NA_QROWS = 4
NA_TILE = NA_QROWS * GRID_W
NA_KROWS = 12


def _params(sem):
    return pltpu.CompilerParams(dimension_semantics=sem, vmem_limit_bytes=VMEM_LIMIT_BYTES)


def _rot(h, c, s_up, s_dn, shift):
    return h * c + pltpu.roll(h, LANES - shift, 1) * s_up + pltpu.roll(h, shift, 1) * s_dn


def _inproj_kernel(x_ref, w_ref, ca_ref, ua_ref, da_ref, cb_ref, ub_ref, db_ref, gq_ref, gk_ref,
                   o_ref, *, tn, nchunk):
    j = pl.program_id(1)
    rc = x_ref.shape[0] // nchunk
    heads = tn // HEAD_DIM
    is_rope_a = j < OFF_VA // tn
    is_qb = (j >= OFF_QB // tn) & (j < OFF_KB // tn)
    is_kb = (j >= OFF_KB // tn) & (j < OFF_VB // tn)
    is_plain = jnp.logical_not(is_rope_a | is_qb | is_kb)
    is_q = (j < OFF_KA // tn) | is_qb | ((j >= OFF_QC // tn) & (j < OFF_KC // tn))
    qs = jnp.where(is_q, Q_SCALE, 1.0).astype(F32)

    def run(epilogue):
        for c in range(nchunk):
            rows = slice(c * rc, (c + 1) * rc)
            acc = jnp.dot(x_ref[rows, :], w_ref[...], preferred_element_type=F32)
            for hh in range(heads):
                sl = slice(hh * HEAD_DIM, (hh + 1) * HEAD_DIM)
                o_ref[rows, sl] = (epilogue(acc[:, sl], rows) * qs).astype(o_ref.dtype)

    def rope_a(h, rows):
        return _rot(h, ca_ref[rows, :], ua_ref[rows, :], da_ref[rows, :], ROPE_DIM // 2)

    def norm_rope_b(g_ref):
        def f(h, rows):
            h = h * lax.rsqrt(jnp.mean(h * h, axis=-1, keepdims=True) + RMS_EPS) * g_ref[...]
            return _rot(h, cb_ref[rows, :], ub_ref[rows, :], db_ref[rows, :], HEAD_DIM // 4)
        return f

    pl.when(is_plain)(lambda: run(lambda h, rows: h))
    pl.when(is_rope_a)(lambda: run(rope_a))
    pl.when(is_qb)(lambda: run(norm_rope_b(gq_ref)))
    pl.when(is_kb)(lambda: run(norm_rope_b(gk_ref)))


def _w_spec(w):
    _, k, tn = w.shape
    return pl.BlockSpec((None, k, tn), lambda i, j: (j, 0, 0))


def _in_proj(xb, w, tabs, gq, gk, seq, tm=1024, nchunk=4):
    m, k = xb.shape
    nt, _, tn = w.shape
    assert m % tm == 0 and seq % tm == 0 and tm % nchunk == 0
    for off in (OFF_KA, OFF_VA, OFF_QB, OFF_KB, OFF_VB, OFF_QC, OFF_KC):
        assert off % tn == 0
    nseq = seq // tm
    tab_spec = pl.BlockSpec((tm, LANES), lambda i, j: (i % nseq, 0))
    g_spec = pl.BlockSpec((1, LANES), lambda i, j: (0, 0))
    return pl.pallas_call(
        functools.partial(_inproj_kernel, tn=tn, nchunk=nchunk),
        grid=(m // tm, nt),
        in_specs=[pl.BlockSpec((tm, k), lambda i, j: (i, 0)), _w_spec(w)] + [tab_spec] * 6 + [g_spec, g_spec],
        out_specs=pl.BlockSpec((tm, tn), lambda i, j: (i, j)),
        out_shape=jax.ShapeDtypeStruct((m, nt * tn), BF16),
        compiler_params=_params(("parallel", "arbitrary")),
        name="in_proj",
    )(xb, w, *tabs, gq, gk)


def _cast_kernel(w_ref, o_ref):
    o_ref[...] = w_ref[...].astype(o_ref.dtype)


def _cast_tiled(w3, layer, tn):
    _, k, n = w3.shape
    assert n % tn == 0
    return pl.pallas_call(
        _cast_kernel,
        grid=(n // tn,),
        in_specs=[pl.BlockSpec((None, k, tn), lambda j: (layer, 0, j))],
        out_specs=pl.BlockSpec((None, k, tn), lambda j: (j, 0, 0)),
        out_shape=jax.ShapeDtypeStruct((n // tn, k, tn), BF16),
        compiler_params=_params(("parallel",)),
        name="cast_weight",
    )(w3)


def _rope_tables(seq):
    def cs(pos, theta, d):
        half = d // 2
        inv = theta ** (-jnp.arange(half, dtype=F32) * 2.0 / d)
        ang = pos.astype(F32)[:, None] * inv[None, :]
        return jnp.cos(ang), jnp.sin(ang)

    t = jnp.arange(seq)
    ca, sa = cs(t, ROPE_THETA, ROPE_DIM)
    pad = HEAD_DIM - ROPE_DIM
    z = lambda w_: jnp.zeros((seq, w_), F32)
    tab_ca = jnp.concatenate([ca, ca, jnp.ones((seq, pad), F32)], 1)
    tab_ua = jnp.concatenate([-sa, z(ROPE_DIM // 2 + pad)], 1)
    tab_da = jnp.concatenate([z(ROPE_DIM // 2), sa, z(pad)], 1)
    cr, sr = cs(t // GRID_W, AXIAL_THETA, HEAD_DIM // 2)
    cc, sc = cs(t % GRID_W, AXIAL_THETA, HEAD_DIM // 2)
    q = HEAD_DIM // 4
    tab_cb = jnp.concatenate([cr, cr, cc, cc], 1)
    tab_ub = jnp.concatenate([-sr, z(q), -sc, z(q)], 1)
    tab_db = jnp.concatenate([z(q), sr, z(q), sc], 1)
    return (tab_ca, tab_ua, tab_da, tab_cb, tab_ub, tab_db)


def _mm_resid_kernel(x_ref, w_ref, r_ref, o_ref, *, alpha):
    o_ref[...] = alpha * r_ref[...] + jnp.dot(x_ref[...], w_ref[...], preferred_element_type=F32)


def _ln_apply(y, mu, rs, g, b):
    reps = y.shape[1] // LANES
    mu = jnp.concatenate([mu] * reps, axis=1)
    rs = jnp.concatenate([rs] * reps, axis=1)
    return (y - mu) * rs * g + b


def _mm_resid_ln_kernel(x_ref, w_ref, y_ref, mu_ref, rs_ref, g_ref, b_ref, o_ref, *, alpha):
    resid = _ln_apply(y_ref[...], mu_ref[...], rs_ref[...], g_ref[...], b_ref[...])
    o_ref[...] = alpha * resid + jnp.dot(x_ref[...], w_ref[...], preferred_element_type=F32)


def _mm_resid(xb, w, resid, alpha, tm):
    m, k = xb.shape
    nt, _, tn = w.shape
    assert m % tm == 0
    tile = pl.BlockSpec((tm, tn), lambda i, j: (i, j))
    if isinstance(resid, tuple):
        y, mu, rs, g, b = resid
        stat = pl.BlockSpec((tm, LANES), lambda i, j: (i, 0))
        vec = pl.BlockSpec((1, tn), lambda i, j: (0, j))
        body = _mm_resid_ln_kernel
        extra, extra_specs = (y, mu, rs, g.reshape(1, -1), b.reshape(1, -1)), [tile, stat, stat, vec, vec]
    else:
        body = _mm_resid_kernel
        extra, extra_specs = (resid,), [tile]
    return pl.pallas_call(
        functools.partial(body, alpha=alpha),
        grid=(m // tm, nt),
        in_specs=[pl.BlockSpec((tm, k), lambda i, j: (i, 0)), _w_spec(w)] + extra_specs,
        out_specs=tile,
        out_shape=jax.ShapeDtypeStruct((m, nt * tn), F32),
        compiler_params=_params(("parallel", "arbitrary")),
        name="mm_resid",
    )(xb, w, *extra)


def _gateup_kernel(x_ref, wg_ref, wu_ref, o_ref):
    x = x_ref[...]
    g = jnp.dot(x, wg_ref[...], preferred_element_type=F32)
    u = jnp.dot(x, wu_ref[...], preferred_element_type=F32)
    o_ref[...] = (g / (1.0 + jnp.exp(-g)) * u).astype(o_ref.dtype)


def _gate_up(xb, wg, wu, tm):
    m, k = xb.shape
    nt, _, tn = wg.shape
    assert m % tm == 0 and wu.shape == wg.shape
    return pl.pallas_call(
        _gateup_kernel,
        grid=(m // tm, nt),
        in_specs=[pl.BlockSpec((tm, k), lambda i, j: (i, 0)), _w_spec(wg), _w_spec(wu)],
        out_specs=pl.BlockSpec((tm, tn), lambda i, j: (i, j)),
        out_shape=jax.ShapeDtypeStruct((m, nt * tn), BF16),
        compiler_params=_params(("parallel", "arbitrary")),
        name="gate_up",
    )(xb, wg, wu)


def _ln_stats(y):
    mu = jnp.mean(y, axis=-1, keepdims=True)
    d = y - mu
    var = jnp.mean(d * d, axis=-1, keepdims=True)
    return mu, lax.rsqrt(var + LN_EPS)


def _ln_final_kernel(y_ref, g_ref, b_ref, o_ref):
    y = y_ref[...]
    mu, rs = _ln_stats(y)
    o_ref[...] = (y - mu) * rs * g_ref[...] + b_ref[...]


def _ln_bf16_kernel(y_ref, g_ref, b_ref, ob_ref, mu_ref, rs_ref):
    y = y_ref[...]
    mu, rs = _ln_stats(y)
    ob_ref[...] = ((y - mu) * rs * g_ref[...] + b_ref[...]).astype(ob_ref.dtype)
    mu_ref[...] = jnp.broadcast_to(mu, mu_ref.shape)
    rs_ref[...] = jnp.broadcast_to(rs, rs_ref.shape)


def _layer_norm(y, g, b, final, tm=256):
    m, d = y.shape
    row = pl.BlockSpec((tm, d), lambda i: (i, 0))
    vec = pl.BlockSpec((1, d), lambda i: (0, 0))
    stat = pl.BlockSpec((tm, LANES), lambda i: (i, 0))
    if final:
        body, out_specs, out_shape = _ln_final_kernel, row, jax.ShapeDtypeStruct((m, d), F32)
    else:
        body, out_specs = _ln_bf16_kernel, [row, stat, stat]
        out_shape = [jax.ShapeDtypeStruct((m, d), BF16)] + [jax.ShapeDtypeStruct((m, LANES), F32)] * 2
    return pl.pallas_call(
        body,
        grid=(m // tm,),
        in_specs=[row, vec, vec],
        out_specs=out_specs,
        out_shape=out_shape,
        compiler_params=_params(("parallel",)),
        name="layer_norm",
    )(y, g.reshape(1, d), b.reshape(1, d))


def _mix_kernel(oa_ref, ob_ref, oc_ref, g_ref, o_ref):
    off = 0
    for ref in (oa_ref, ob_ref, oc_ref):
        w = ref.shape[1]
        x = ref[...].astype(F32)
        xn = x * lax.rsqrt(jnp.mean(x * x, axis=-1, keepdims=True) + RMS_EPS)
        o_ref[:, off:off + w] = (xn * g_ref[:, off:off + w]).astype(o_ref.dtype)
        off += w


def _mix(oa, ob, oc, g, tm=512):
    m = oa.shape[0]
    d = oa.shape[1] + ob.shape[1] + oc.shape[1]
    return pl.pallas_call(
        _mix_kernel,
        grid=(m // tm,),
        in_specs=[pl.BlockSpec((tm, oa.shape[1]), lambda i: (i, 0)),
                  pl.BlockSpec((tm, ob.shape[1]), lambda i: (i, 0)),
                  pl.BlockSpec((tm, oc.shape[1]), lambda i: (i, 0)),
                  pl.BlockSpec((1, d), lambda i: (0, 0))],
        out_specs=pl.BlockSpec((tm, d), lambda i: (i, 0)),
        out_shape=jax.ShapeDtypeStruct((m, d), BF16),
        compiler_params=_params(("parallel",)),
        name="group_rms_mix",
    )(oa, ob, oc, g.reshape(1, d))


def _qk(q, k):
    return lax.dot_general(q, k, (((1,), (1,)), ((), ())), preferred_element_type=F32)


def _exp2_rows(s, m):
    return jnp.concatenate([jnp.exp2(s[:, c * LANES:(c + 1) * LANES] - m) for c in range(s.shape[1] // LANES)],
                           axis=1)


ONES_ROWS = 16
QK_AHEAD = 4
BAND_AHEAD = 2


def _attn_b_kernel(q_ref, k_ref, v_ref, o_ref, m_ref, acc_ref, vxt_ref, *, chain):
    kv = pl.program_id(3)
    tq = q_ref.shape[0]
    tk = k_ref.shape[0]

    @pl.when(kv == 0)
    def _():
        m_ref[...] = jnp.full(m_ref.shape, -jnp.inf, F32)
        acc_ref[...] = jnp.zeros(acc_ref.shape, F32)
        vxt_ref[HEAD_DIM:, :] = jnp.ones((ONES_ROWS, tk), vxt_ref.dtype)

    vxt_ref[:HEAD_DIM, :] = v_ref[...].astype(F32).T.astype(vxt_ref.dtype)
    k = k_ref[...]
    vxt = vxt_ref[...]
    nc = tq // chain
    chains = [(r, c) for r in range(GRP_B) for c in range(nc)]

    def scores(r, c):
        return _qk(k, q_ref[c * chain:(c + 1) * chain, r * HEAD_DIM:(r + 1) * HEAD_DIM])

    pending = [scores(*chains[i]) for i in range(min(QK_AHEAD, len(chains)))]
    for ch in range(len(chains)):
        st = pending.pop(0)
        if ch + QK_AHEAD < len(chains):
            pending.append(scores(*chains[ch + QK_AHEAD]))
        m_prev = m_ref[ch * 8:(ch + 1) * 8, :]
        m_new = jnp.maximum(m_prev, jnp.max(st, axis=0, keepdims=True))
        alpha = jnp.exp2(m_prev - m_new)[0:1, :]
        p = jnp.exp2(st - m_new[0:1, :]).astype(vxt.dtype)
        acc_ref[ch] = alpha * acc_ref[ch] + jnp.dot(vxt, p, preferred_element_type=F32)
        m_ref[ch * 8:(ch + 1) * 8, :] = m_new

    @pl.when(kv == pl.num_programs(3) - 1)
    def _():
        for ch, (r, c) in enumerate(chains):
            a = acc_ref[ch]
            o = (a[:HEAD_DIM, :] / a[HEAD_DIM:HEAD_DIM + 1, :]).T
            o_ref[c * chain:(c + 1) * chain, r * HEAD_DIM:(r + 1) * HEAD_DIM] = o.astype(o_ref.dtype)


def _attn_b(proj3, tq=2048, tk=2048, chain=256):
    b, s, _ = proj3.shape
    qw = GRP_B * HEAD_DIM
    assert s % tq == 0 and s % tk == 0 and tq % chain == 0 and OFF_QB % qw == 0
    nchain = GRP_B * tq // chain
    return pl.pallas_call(
        functools.partial(_attn_b_kernel, chain=chain),
        grid=(b, HKV_B, s // tq, s // tk),
        in_specs=[pl.BlockSpec((None, tq, qw), lambda bi, g, qi, ki: (bi, qi, OFF_QB // qw + g)),
                  pl.BlockSpec((None, tk, HEAD_DIM), lambda bi, g, qi, ki: (bi, ki, OFF_KB // HEAD_DIM + g)),
                  pl.BlockSpec((None, tk, HEAD_DIM), lambda bi, g, qi, ki: (bi, ki, OFF_VB // HEAD_DIM + g))],
        out_specs=pl.BlockSpec((None, tq, qw), lambda bi, g, qi, ki: (bi, qi, g)),
        out_shape=jax.ShapeDtypeStruct((b, s, W_BQ), BF16),
        scratch_shapes=[pltpu.VMEM((nchain * 8, chain), F32),
                        pltpu.VMEM((nchain, HEAD_DIM + ONES_ROWS, chain), F32),
                        pltpu.VMEM((HEAD_DIM + ONES_ROWS, tk), BF16)],
        compiler_params=_params(("parallel", "parallel", "parallel", "arbitrary")),
        name="attn_axial_gqa",
    )(proj3, proj3, proj3)


def _band_bias_table():
    nt = 2 * BAND_REACH // BAND_TILE + 1
    u = np.arange(nt + 1)[:, None, None] - BAND_REACH // BAND_TILE
    rel = u * BAND_TILE + np.arange(BAND_TILE)[None, None, :] - np.arange(BAND_TILE)[None, :, None]
    mult = np.zeros(rel.shape, np.int64)
    for window, dil in DILATED_PATTERNS:
        mult += (np.abs(rel) <= window // 2) & (rel % dil == 0)
    mult[nt] = 0
    return np.where(mult > 0, np.log2(np.maximum(mult, 1)), NEG_INF).astype(np.float32)


def _attn_a_kernel(q_ref, k_ref, v_ref, tbl_ref, o_ref, vx_ref, *, nblk, wblk, heads, qsub):
    t = BAND_TILE
    r = BAND_REACH // t
    nt = 2 * r + 1
    qi = pl.program_id(2)

    @pl.when(qi == 0)
    def _():
        vx_ref[:, :, HEAD_DIM:] = jnp.ones((qsub * heads, wblk * t, HEAD_DIM), vx_ref.dtype)

    def scores(u, hh):
        qt = qi * qsub + u
        sb = jnp.clip(qt - r, 0, nblk - wblk)
        hs = slice(hh * HEAD_DIM, (hh + 1) * HEAD_DIM)
        q = q_ref[u * t:(u + 1) * t, hs]
        s = []
        for c in range(wblk):
            d = sb + c - qt + r
            rows = pl.ds(pl.multiple_of((sb + c) * t, t), t)
            vx_ref[u * heads + hh, c * t:(c + 1) * t, :HEAD_DIM] = v_ref[rows, hs]
            s.append(_qk(q, k_ref[rows, hs]) + tbl_ref[jnp.where((d >= 0) & (d < nt), d, nt)])
        return s

    chains = [(u, hh) for u in range(qsub) for hh in range(heads)]
    pending = [scores(*chains[i]) for i in range(min(BAND_AHEAD, len(chains)))]
    for ch, (u, hh) in enumerate(chains):
        s = pending.pop(0)
        if ch + BAND_AHEAD < len(chains):
            pending.append(scores(*chains[ch + BAND_AHEAD]))
        m = functools.reduce(jnp.maximum, s)
        m = functools.reduce(jnp.maximum, [m[:, c * LANES:(c + 1) * LANES] for c in range(t // LANES)])
        m = jnp.broadcast_to(jnp.max(m, axis=-1, keepdims=True), (t, LANES))
        pv = jnp.zeros((t, 2 * HEAD_DIM), F32)
        for c in range(wblk):
            p = _exp2_rows(s[c], m).astype(vx_ref.dtype)
            pv = pv + jnp.dot(p, vx_ref[ch, c * t:(c + 1) * t, :], preferred_element_type=F32)
        o_ref[u * t:(u + 1) * t, hh * HEAD_DIM:(hh + 1) * HEAD_DIM] = (
            pv[:, :HEAD_DIM] / pv[:, HEAD_DIM:]).astype(o_ref.dtype)


def _attn_a(proj3, band_tbl, heads=2, qsub=4):
    b, s, _ = proj3.shape
    t = BAND_TILE
    nblk = s // t
    wblk = min(nblk, 2 * BAND_REACH // t + 1)
    hw = heads * HEAD_DIM
    assert s % (qsub * t) == 0 and H_A % heads == 0 and OFF_KA % hw == 0 and OFF_VA % hw == 0
    kv_spec = lambda off: pl.BlockSpec((None, s, hw), lambda bi, h, qi: (bi, 0, off // hw + h),
                                       pipeline_mode=pl.Buffered(1))
    return pl.pallas_call(
        functools.partial(_attn_a_kernel, nblk=nblk, wblk=wblk, heads=heads, qsub=qsub),
        grid=(b, H_A // heads, nblk // qsub),
        in_specs=[pl.BlockSpec((None, qsub * t, hw), lambda bi, h, qi: (bi, qi, OFF_QA // hw + h)),
                  kv_spec(OFF_KA), kv_spec(OFF_VA),
                  pl.BlockSpec(band_tbl.shape, lambda bi, h, qi: (0, 0, 0), pipeline_mode=pl.Buffered(1))],
        out_specs=pl.BlockSpec((None, qsub * t, hw), lambda bi, h, qi: (bi, qi, h)),
        out_shape=jax.ShapeDtypeStruct((b, s, W_A), BF16),
        scratch_shapes=[pltpu.VMEM((qsub * heads, wblk * t, 2 * HEAD_DIM), BF16)],
        compiler_params=_params(("parallel", "parallel", "arbitrary")),
        name="attn_dilated",
    )(proj3, proj3, proj3, band_tbl)


def _rpb_table_kernel(rpb_ref, o_ref):
    h = pl.program_id(0)
    a = pl.program_id(1) - 1
    n_roff, n_coff = 2 * NA_ROWS - 1, 2 * NA_COLS - 1
    shape = (GRID_W, 2 * GRID_W)
    qc = lax.broadcasted_iota(jnp.int32, shape, 0)
    lane = lax.broadcasted_iota(jnp.int32, shape, 1)
    hi = lane >= GRID_W
    kc = jnp.where(hi, lane - GRID_W, lane)
    coff = jnp.clip(kc - qc, -(NA_COLS - 1), NA_COLS - 1) + (NA_COLS - 1)
    cstart = jnp.clip(qc - NA_COLS // 2, 0, GRID_W - NA_COLS)
    cmask = (kc >= cstart) & (kc < cstart + NA_COLS)
    a_lo = jnp.clip(a, 0, n_roff - 1)
    a_hi = jnp.clip(a + 1, 0, n_roff - 1)
    val = jnp.zeros(shape, F32)
    for bb in range(n_coff):
        r_lo = rpb_ref[(h * n_roff + a_lo) * n_coff + bb]
        r_hi = rpb_ref[(h * n_roff + a_hi) * n_coff + bb]
        val = jnp.where(coff == bb, jnp.where(hi, r_hi, r_lo), val)
    ok_lo = ((a >= 0) & (a < n_roff)).astype(jnp.int32)
    ok_hi = ((a + 1 >= 0) & (a + 1 < n_roff)).astype(jnp.int32)
    ok = jnp.where(hi, ok_hi, ok_lo) > 0
    o_ref[0, 0] = jnp.where(cmask & ok, val * LOG2_E, NEG_INF)


def _rpb_table(rpb):
    h = rpb.shape[0]
    n_slots = 2 * NA_ROWS
    return pl.pallas_call(
        _rpb_table_kernel,
        grid=(h, n_slots),
        in_specs=[pl.BlockSpec(memory_space=pltpu.SMEM)],
        out_specs=pl.BlockSpec((1, 1, GRID_W, 2 * GRID_W), lambda hi, ai: (hi, ai, 0, 0)),
        out_shape=jax.ShapeDtypeStruct((h, n_slots, GRID_W, 2 * GRID_W), F32),
        compiler_params=_params(("arbitrary", "arbitrary")),
        name="rpb_table",
    )(rpb.reshape(-1))


def _attn_c_kernel(q_ref, k_ref, v_ref, tc_ref, o_ref, vx_ref, *, rows, heads, qsub):
    nb = rows // NA_QROWS
    kblk = NA_KROWS // NA_QROWS
    nk = NA_KROWS * GRID_W

    @pl.when(pl.program_id(2) == 0)
    def _():
        vx_ref[:, :, HEAD_DIM:] = jnp.ones((qsub * heads, nk, HEAD_DIM), vx_ref.dtype)

    lane = lax.broadcasted_iota(jnp.int32, (GRID_W, 2 * GRID_W), 1)
    for u in range(qsub):
        b = pl.program_id(2) * qsub + u
        kb = jnp.clip(b - 1, 0, nb - kblk)
        win = pl.ds(pl.multiple_of(kb * NA_TILE, NA_TILE), nk)
        qrows = slice(u * NA_TILE, (u + 1) * NA_TILE)
        sel = []
        for ql in range(NA_QROWS):
            i = b * NA_QROWS + ql
            rstart = jnp.clip(i - NA_ROWS // 2, 0, rows - NA_ROWS)
            for pr in range(NA_KROWS // 2):
                kr = kb * NA_QROWS + 2 * pr
                a = kr - i + (NA_ROWS - 1)
                v_lo = ((kr >= rstart) & (kr < rstart + NA_ROWS)).astype(jnp.int32)
                v_hi = ((kr + 1 >= rstart) & (kr + 1 < rstart + NA_ROWS)).astype(jnp.int32)
                sel.append((jnp.where(lane < GRID_W, v_lo, v_hi) > 0, jnp.clip(a + 1, 0, 2 * NA_ROWS - 1)))
        for hh in range(heads):
            ch = u * heads + hh
            hs = slice(hh * HEAD_DIM, (hh + 1) * HEAD_DIM)
            vx_ref[ch, :, :HEAD_DIM] = v_ref[win, hs]
            strips = []
            for ql in range(NA_QROWS):
                pieces = []
                for pr in range(NA_KROWS // 2):
                    valid, slot = sel[ql * (NA_KROWS // 2) + pr]
                    pieces.append(jnp.where(valid, tc_ref[hh, slot], NEG_INF))
                strips.append(jnp.concatenate(pieces, axis=1))
            s = _qk(q_ref[qrows, hs], k_ref[win, hs]) + jnp.concatenate(strips, axis=0)
            m = functools.reduce(jnp.maximum, [s[:, c * LANES:(c + 1) * LANES] for c in range(nk // LANES)])
            m = jnp.broadcast_to(jnp.max(m, axis=-1, keepdims=True), (NA_TILE, LANES))
            p = _exp2_rows(s, m).astype(vx_ref.dtype)
            pv = jnp.dot(p, vx_ref[ch], preferred_element_type=F32)
            o_ref[qrows, hs] = (pv[:, :HEAD_DIM] / pv[:, HEAD_DIM:]).astype(o_ref.dtype)


def _attn_c(proj3, tc, heads=2, qsub=8):
    b, s, _ = proj3.shape
    rows = s // GRID_W
    hw = heads * HEAD_DIM
    tq = qsub * NA_TILE
    assert s % tq == 0 and rows % NA_QROWS == 0 and rows >= NA_KROWS
    assert H_C % heads == 0 and OFF_QC % hw == 0 and OFF_KC % hw == 0 and OFF_VC % hw == 0
    kv_spec = lambda off: pl.BlockSpec((None, s, hw), lambda bi, h, qi: (bi, 0, off // hw + h),
                                       pipeline_mode=pl.Buffered(1))
    return pl.pallas_call(
        functools.partial(_attn_c_kernel, rows=rows, heads=heads, qsub=qsub),
        grid=(b, H_C // heads, s // tq),
        in_specs=[pl.BlockSpec((None, tq, hw), lambda bi, h, qi: (bi, qi, OFF_QC // hw + h)),
                  kv_spec(OFF_KC), kv_spec(OFF_VC),
                  pl.BlockSpec((heads, 2 * NA_ROWS, GRID_W, 2 * GRID_W), lambda bi, h, qi: (h, 0, 0, 0))],
        out_specs=pl.BlockSpec((None, tq, hw), lambda bi, h, qi: (bi, qi, h)),
        out_shape=jax.ShapeDtypeStruct((b, s, W_C), BF16),
        scratch_shapes=[pltpu.VMEM((qsub * heads, NA_KROWS * GRID_W, 2 * HEAD_DIM), BF16)],
        compiler_params=_params(("parallel", "parallel", "arbitrary")),
        name="attn_neighbourhood",
    )(proj3, proj3, proj3, tc)


def _layer(x, xb, b, s, lw, tabs, band_tbl, last):
    m = xb.shape[0]
    proj = _in_proj(xb, lw["w_in"], tabs, lw["g_qn"], lw["g_kn"], s)
    proj3 = proj.reshape(b, s, IN_WIDTH)
    oa = _attn_a(proj3, band_tbl).reshape(m, W_A)
    ob = _attn_b(proj3).reshape(m, W_BQ)
    oc = _attn_c(proj3, lw["tc"]).reshape(m, W_C)
    mixed = _mix(oa, ob, oc, lw["g_grp"])
    y1 = _mm_resid(mixed, lw["w_out"], x, lw["alpha"], tm=1024)
    xb1, mu1, rs1 = _layer_norm(y1, lw["ln1_g"], lw["ln1_b"], final=False)
    hdn = _gate_up(xb1, lw["w_gate"], lw["w_up"], tm=1024)
    y2 = _mm_resid(hdn, lw["w_down"], (y1, mu1, rs1, lw["ln1_g"], lw["ln1_b"]), lw["alpha"], tm=512)
    if last:
        return _layer_norm(y2, lw["ln2_g"], lw["ln2_b"], final=True), None
    xb2, mu2, rs2 = _layer_norm(y2, lw["ln2_g"], lw["ln2_b"], final=False)
    return (y2, mu2, rs2, lw["ln2_g"], lw["ln2_b"]), xb2


def kernel(x_prompt, x_sample, w_in, g_qn, g_kn, rpb, g_grp, w_out, ln1_g, ln1_b, w_gate, w_up, w_down,
           ln2_g, ln2_b):
    depth = w_in.shape[0]
    layers = []
    for l in range(depth):
        layers.append(dict(
            w_in=_cast_tiled(w_in, l, 512), w_out=_cast_tiled(w_out, l, 512),
            w_gate=_cast_tiled(w_gate, l, 256), w_up=_cast_tiled(w_up, l, 256),
            w_down=_cast_tiled(w_down, l, 256),
            g_qn=g_qn[l].reshape(1, HEAD_DIM), g_kn=g_kn[l].reshape(1, HEAD_DIM),
            tc=_rpb_table(rpb[l]), g_grp=g_grp[l], alpha=(2.0 * depth) ** 0.25,
            ln1_g=ln1_g[l], ln1_b=ln1_b[l], ln2_g=ln2_g[l], ln2_b=ln2_b[l]))
    band_tbl = jnp.asarray(_band_bias_table())
    tabs = _rope_tables(max(x_prompt.shape[1], x_sample.shape[1]))

    def trunk(x3):
        b, s, d = x3.shape
        x = x3.reshape(b * s, d)
        xb = x.astype(BF16)
        for l, lw in enumerate(layers):
            x, xb = _layer(x, xb, b, s, lw, tabs, band_tbl, last=l == depth - 1)
        return x.reshape(b, s, d)

    return (trunk(x_prompt), trunk(x_sample))
```

```python
import functools

import numpy as np
import jax
import jax.numpy as jnp
from jax import lax
from jax.experimental import pallas as pl
from jax.experimental.pallas import tpu as pltpu

F32 = jnp.float32
BF16 = jnp.bfloat16

HEAD_DIM = 128
H_A, H_B, HKV_B, H_C = 8, 16, 4, 8
GRP_B = H_B // HKV_B
DILATED_PATTERNS = ((128, 1), (512, 4), (2048, 16))
ROPE_THETA = 500000.0
ROPE_DIM = HEAD_DIM // 4
AXIAL_THETA = 10000.0
GRID_W = 64
NA_ROWS = 8
NA_COLS = 16
LN_EPS = 1e-5
RMS_EPS = 1e-6
NEG_INF = -1e30
LOG2_E = 1.4426950408889634
Q_SCALE = HEAD_DIM ** -0.5 * LOG2_E

W_A = H_A * HEAD_DIM
W_BQ = H_B * HEAD_DIM
W_BKV = HKV_B * HEAD_DIM
W_C = H_C * HEAD_DIM
OFF_QA, OFF_KA, OFF_VA = 0, W_A, 2 * W_A
OFF_QB = 3 * W_A
OFF_KB = OFF_QB + W_BQ
OFF_VB = OFF_KB + W_BKV
OFF_QC = OFF_VB + W_BKV
OFF_KC = OFF_QC + W_C
OFF_VC = OFF_KC + W_C
IN_WIDTH = OFF_VC + W_C

LANES = 128
VMEM_LIMIT_BYTES = 52 * 1024 * 1024

BAND_REACH = max(w // 2 for w, _ in DILATED_PATTERNS)
BAND_TILE = 256
NA_QROWS = 4
NA_TILE = NA_QROWS * GRID_W
NA_KROWS = 12


def _params(sem):
    return pltpu.CompilerParams(dimension_semantics=sem, vmem_limit_bytes=VMEM_LIMIT_BYTES)


def _rot(h, c, s_up, s_dn, shift):
    return h * c + pltpu.roll(h, LANES - shift, 1) * s_up + pltpu.roll(h, shift, 1) * s_dn


def _inproj_kernel(x_ref, w_ref, ca_ref, ua_ref, da_ref, cb_ref, ub_ref, db_ref, gq_ref, gk_ref,
                   o_ref, *, tn, nchunk):
    j = pl.program_id(1)
    rc = x_ref.shape[0] // nchunk
    heads = tn // HEAD_DIM
    is_rope_a = j < OFF_VA // tn
    is_qb = (j >= OFF_QB // tn) & (j < OFF_KB // tn)
    is_kb = (j >= OFF_KB // tn) & (j < OFF_VB // tn)
    is_plain = jnp.logical_not(is_rope_a | is_qb | is_kb)
    is_q = (j < OFF_KA // tn) | is_qb | ((j >= OFF_QC // tn) & (j < OFF_KC // tn))
    qs = jnp.where(is_q, Q_SCALE, 1.0).astype(F32)

    def run(epilogue):
        for c in range(nchunk):
            rows = slice(c * rc, (c + 1) * rc)
            acc = jnp.dot(x_ref[rows, :], w_ref[...], preferred_element_type=F32)
            for hh in range(heads):
                sl = slice(hh * HEAD_DIM, (hh + 1) * HEAD_DIM)
                o_ref[rows, sl] = (epilogue(acc[:, sl], rows) * qs).astype(o_ref.dtype)

    def rope_a(h, rows):
        return _rot(h, ca_ref[rows, :], ua_ref[rows, :], da_ref[rows, :], ROPE_DIM // 2)

    def norm_rope_b(g_ref):
        def f(h, rows):
            h = h * lax.rsqrt(jnp.mean(h * h, axis=-1, keepdims=True) + RMS_EPS) * g_ref[...]
            return _rot(h, cb_ref[rows, :], ub_ref[rows, :], db_ref[rows, :], HEAD_DIM // 4)
        return f

    pl.when(is_plain)(lambda: run(lambda h, rows: h))
    pl.when(is_rope_a)(lambda: run(rope_a))
    pl.when(is_qb)(lambda: run(norm_rope_b(gq_ref)))
    pl.when(is_kb)(lambda: run(norm_rope_b(gk_ref)))


def _w_spec(w):
    _, k, tn = w.shape
    return pl.BlockSpec((None, k, tn), lambda i, j: (j, 0, 0))


def _in_proj(xb, w, tabs, gq, gk, seq, tm=1024, nchunk=4):
    m, k = xb.shape
    nt, _, tn = w.shape
    assert m % tm == 0 and seq % tm == 0 and tm % nchunk == 0
    for off in (OFF_KA, OFF_VA, OFF_QB, OFF_KB, OFF_VB, OFF_QC, OFF_KC):
        assert off % tn == 0
    nseq = seq // tm
    tab_spec = pl.BlockSpec((tm, LANES), lambda i, j: (i % nseq, 0))
    g_spec = pl.BlockSpec((1, LANES), lambda i, j: (0, 0))
    return pl.pallas_call(
        functools.partial(_inproj_kernel, tn=tn, nchunk=nchunk),
        grid=(m // tm, nt),
        in_specs=[pl.BlockSpec((tm, k), lambda i, j: (i, 0)), _w_spec(w)] + [tab_spec] * 6 + [g_spec, g_spec],
        out_specs=pl.BlockSpec((tm, tn), lambda i, j: (i, j)),
        out_shape=jax.ShapeDtypeStruct((m, nt * tn), BF16),
        compiler_params=_params(("parallel", "arbitrary")),
        name="in_proj",
    )(xb, w, *tabs, gq, gk)


def _cast_kernel(w_ref, o_ref):
    o_ref[...] = w_ref[...].astype(o_ref.dtype)


def _cast_tiled(w3, layer, tn):
    _, k, n = w3.shape
    assert n % tn == 0
    return pl.pallas_call(
        _cast_kernel,
        grid=(n // tn,),
        in_specs=[pl.BlockSpec((None, k, tn), lambda j: (layer, 0, j))],
        out_specs=pl.BlockSpec((None, k, tn), lambda j: (j, 0, 0)),
        out_shape=jax.ShapeDtypeStruct((n // tn, k, tn), BF16),
        compiler_params=_params(("parallel",)),
        name="cast_weight",
    )(w3)


def _rope_tables(seq):
    def cs(pos, theta, d):
        half = d // 2
        inv = theta ** (-jnp.arange(half, dtype=F32) * 2.0 / d)
        ang = pos.astype(F32)[:, None] * inv[None, :]
        return jnp.cos(ang), jnp.sin(ang)

    t = jnp.arange(seq)
    ca, sa = cs(t, ROPE_THETA, ROPE_DIM)
    pad = HEAD_DIM - ROPE_DIM
    z = lambda w_: jnp.zeros((seq, w_), F32)
    tab_ca = jnp.concatenate([ca, ca, jnp.ones((seq, pad), F32)], 1)
    tab_ua = jnp.concatenate([-sa, z(ROPE_DIM // 2 + pad)], 1)
    tab_da = jnp.concatenate([z(ROPE_DIM // 2), sa, z(pad)], 1)
    cr, sr = cs(t // GRID_W, AXIAL_THETA, HEAD_DIM // 2)
    cc, sc = cs(t % GRID_W, AXIAL_THETA, HEAD_DIM // 2)
    q = HEAD_DIM // 4
    tab_cb = jnp.concatenate([cr, cr, cc, cc], 1)
    tab_ub = jnp.concatenate([-sr, z(q), -sc, z(q)], 1)
    tab_db = jnp.concatenate([z(q), sr, z(q), sc], 1)
    return (tab_ca, tab_ua, tab_da, tab_cb, tab_ub, tab_db)


def _mm_resid_kernel(x_ref, w_ref, r_ref, o_ref, *, alpha):
    o_ref[...] = alpha * r_ref[...] + jnp.dot(x_ref[...], w_ref[...], preferred_element_type=F32)


def _ln_apply(y, mu, rs, g, b):
    reps = y.shape[1] // LANES
    mu = jnp.concatenate([mu] * reps, axis=1)
    rs = jnp.concatenate([rs] * reps, axis=1)
    return (y - mu) * rs * g + b


def _mm_resid_ln_kernel(x_ref, w_ref, y_ref, mu_ref, rs_ref, g_ref, b_ref, o_ref, *, alpha):
    resid = _ln_apply(y_ref[...], mu_ref[...], rs_ref[...], g_ref[...], b_ref[...])
    o_ref[...] = alpha * resid + jnp.dot(x_ref[...], w_ref[...], preferred_element_type=F32)


def _mm_resid(xb, w, resid, alpha, tm):
    m, k = xb.shape
    nt, _, tn = w.shape
    assert m % tm == 0
    tile = pl.BlockSpec((tm, tn), lambda i, j: (i, j))
    if isinstance(resid, tuple):
        y, mu, rs, g, b = resid
        stat = pl.BlockSpec((tm, LANES), lambda i, j: (i, 0))
        vec = pl.BlockSpec((1, tn), lambda i, j: (0, j))
        body = _mm_resid_ln_kernel
        extra, extra_specs = (y, mu, rs, g.reshape(1, -1), b.reshape(1, -1)), [tile, stat, stat, vec, vec]
    else:
        body = _mm_resid_kernel
        extra, extra_specs = (resid,), [tile]
    return pl.pallas_call(
        functools.partial(body, alpha=alpha),
        grid=(m // tm, nt),
        in_specs=[pl.BlockSpec((tm, k), lambda i, j: (i, 0)), _w_spec(w)] + extra_specs,
        out_specs=tile,
        out_shape=jax.ShapeDtypeStruct((m, nt * tn), F32),
        compiler_params=_params(("parallel", "arbitrary")),
        name="mm_resid",
    )(xb, w, *extra)


def _gateup_kernel(x_ref, wg_ref, wu_ref, o_ref):
    x = x_ref[...]
    g = jnp.dot(x, wg_ref[...], preferred_element_type=F32)
    u = jnp.dot(x, wu_ref[...], preferred_element_type=F32)
    o_ref[...] = (g / (1.0 + jnp.exp(-g)) * u).astype(o_ref.dtype)


def _gate_up(xb, wg, wu, tm):
    m, k = xb.shape
    nt, _, tn = wg.shape
    assert m % tm == 0 and wu.shape == wg.shape
    return pl.pallas_call(
        _gateup_kernel,
        grid=(m // tm, nt),
        in_specs=[pl.BlockSpec((tm, k), lambda i, j: (i, 0)), _w_spec(wg), _w_spec(wu)],
        out_specs=pl.BlockSpec((tm, tn), lambda i, j: (i, j)),
        out_shape=jax.ShapeDtypeStruct((m, nt * tn), BF16),
        compiler_params=_params(("parallel", "arbitrary")),
        name="gate_up",
    )(xb, wg, wu)


def _ln_stats(y):
    mu = jnp.mean(y, axis=-1, keepdims=True)
    d = y - mu
    var = jnp.mean(d * d, axis=-1, keepdims=True)
    return mu, lax.rsqrt(var + LN_EPS)


def _ln_final_kernel(y_ref, g_ref, b_ref, o_ref):
    y = y_ref[...]
    mu, rs = _ln_stats(y)
    o_ref[...] = (y - mu) * rs * g_ref[...] + b_ref[...]


def _ln_bf16_kernel(y_ref, g_ref, b_ref, ob_ref, mu_ref, rs_ref):
    y = y_ref[...]
    mu, rs = _ln_stats(y)
    ob_ref[...] = ((y - mu) * rs * g_ref[...] + b_ref[...]).astype(ob_ref.dtype)
    mu_ref[...] = jnp.broadcast_to(mu, mu_ref.shape)
    rs_ref[...] = jnp.broadcast_to(rs, rs_ref.shape)


def _layer_norm(y, g, b, final, tm=256):
    m, d = y.shape
    row = pl.BlockSpec((tm, d), lambda i: (i, 0))
    vec = pl.BlockSpec((1, d), lambda i: (0, 0))
    stat = pl.BlockSpec((tm, LANES), lambda i: (i, 0))
    if final:
        body, out_specs, out_shape = _ln_final_kernel, row, jax.ShapeDtypeStruct((m, d), F32)
    else:
        body, out_specs = _ln_bf16_kernel, [row, stat, stat]
        out_shape = [jax.ShapeDtypeStruct((m, d), BF16)] + [jax.ShapeDtypeStruct((m, LANES), F32)] * 2
    return pl.pallas_call(
        body,
        grid=(m // tm,),
        in_specs=[row, vec, vec],
        out_specs=out_specs,
        out_shape=out_shape,
        compiler_params=_params(("parallel",)),
        name="layer_norm",
    )(y, g.reshape(1, d), b.reshape(1, d))


def _mix_kernel(oa_ref, ob_ref, oc_ref, g_ref, o_ref):
    off = 0
    for ref in (oa_ref, ob_ref, oc_ref):
        w = ref.shape[1]
        x = ref[...].astype(F32)
        xn = x * lax.rsqrt(jnp.mean(x * x, axis=-1, keepdims=True) + RMS_EPS)
        o_ref[:, off:off + w] = (xn * g_ref[:, off:off + w]).astype(o_ref.dtype)
        off += w


def _outproj_kernel(oa_ref, ob_ref, oc_ref, g_ref, w_ref, *rest, alpha, rows_chunk):
    *res, o_ref, lhs_ref = rest

    @pl.when(pl.program_id(1) == 0)
    def _():
        for c in range(lhs_ref.shape[0] // rows_chunk):
            rows = slice(c * rows_chunk, (c + 1) * rows_chunk)
            off = 0
            for ref in (oa_ref, ob_ref, oc_ref):
                w = ref.shape[1]
                x = ref[rows, :].astype(F32)
                xn = x * lax.rsqrt(jnp.mean(x * x, axis=-1, keepdims=True) + RMS_EPS)
                lhs_ref[rows, off:off + w] = (xn * g_ref[:, off:off + w]).astype(lhs_ref.dtype)
                off += w

    if len(res) == 1:
        resid = res[0][...]
    else:
        y_ref, mu_ref, rs_ref, gl_ref, bl_ref = res
        resid = _ln_apply(y_ref[...], mu_ref[...], rs_ref[...], gl_ref[...], bl_ref[...])
    o_ref[...] = alpha * resid + jnp.dot(lhs_ref[...], w_ref[...], preferred_element_type=F32)


def _out_proj(oa, ob, oc, g, w, resid, alpha, tm=1024, rows_chunk=256):
    m = oa.shape[0]
    nt, k, tn = w.shape
    assert m % tm == 0 and tm % rows_chunk == 0 and oa.shape[1] + ob.shape[1] + oc.shape[1] == k
    tile = pl.BlockSpec((tm, tn), lambda i, j: (i, j))
    rowblk = lambda a: pl.BlockSpec((tm, a.shape[1]), lambda i, j: (i, 0))
    if isinstance(resid, tuple):
        y, mu, rs, gl, bl = resid
        stat = pl.BlockSpec((tm, LANES), lambda i, j: (i, 0))
        vec = pl.BlockSpec((1, tn), lambda i, j: (0, j))
        extra, extra_specs = (y, mu, rs, gl.reshape(1, -1), bl.reshape(1, -1)), [tile, stat, stat, vec, vec]
    else:
        extra, extra_specs = (resid,), [tile]
    return pl.pallas_call(
        functools.partial(_outproj_kernel, alpha=alpha, rows_chunk=rows_chunk),
        grid=(m // tm, nt),
        in_specs=[rowblk(oa), rowblk(ob), rowblk(oc), pl.BlockSpec((1, k), lambda i, j: (0, 0)), _w_spec(w)]
        + extra_specs,
        out_specs=tile,
        out_shape=jax.ShapeDtypeStruct((m, nt * tn), F32),
        scratch_shapes=[pltpu.VMEM((tm, k), BF16)],
        compiler_params=_params(("parallel", "arbitrary")),
        name="out_proj",
    )(oa, ob, oc, g.reshape(1, k), w, *extra)


def _mix(oa, ob, oc, g, tm=512):
    m = oa.shape[0]
    d = oa.shape[1] + ob.shape[1] + oc.shape[1]
    return pl.pallas_call(
        _mix_kernel,
        grid=(m // tm,),
        in_specs=[pl.BlockSpec((tm, oa.shape[1]), lambda i: (i, 0)),
                  pl.BlockSpec((tm, ob.shape[1]), lambda i: (i, 0)),
                  pl.BlockSpec((tm, oc.shape[1]), lambda i: (i, 0)),
                  pl.BlockSpec((1, d), lambda i: (0, 0))],
        out_specs=pl.BlockSpec((tm, d), lambda i: (i, 0)),
        out_shape=jax.ShapeDtypeStruct((m, d), BF16),
        compiler_params=_params(("parallel",)),
        name="group_rms_mix",
    )(oa, ob, oc, g.reshape(1, d))


def _qk(q, k):
    return lax.dot_general(q, k, (((1,), (1,)), ((), ())), preferred_element_type=F32)


def _exp2_rows(s, m):
    return jnp.concatenate([jnp.exp2(s[:, c * LANES:(c + 1) * LANES] - m) for c in range(s.shape[1] // LANES)],
                           axis=1)


ONES_ROWS = 16
QK_AHEAD = 4
BAND_AHEAD = 2


def _attn_b_kernel(q_ref, k_ref, v_ref, o_ref, m_ref, acc_ref, vxt_ref, *, chain):
    kv = pl.program_id(3)
    tq = q_ref.shape[0]
    tk = k_ref.shape[0]

    @pl.when(kv == 0)
    def _():
        m_ref[...] = jnp.full(m_ref.shape, -jnp.inf, F32)
        acc_ref[...] = jnp.zeros(acc_ref.shape, F32)
        vxt_ref[HEAD_DIM:, :] = jnp.ones((ONES_ROWS, tk), vxt_ref.dtype)

    vxt_ref[:HEAD_DIM, :] = v_ref[...].astype(F32).T.astype(vxt_ref.dtype)
    k = k_ref[...]
    vxt = vxt_ref[...]
    nc = tq // chain
    chains = [(r, c) for r in range(GRP_B) for c in range(nc)]

    def scores(r, c):
        return _qk(k, q_ref[c * chain:(c + 1) * chain, r * HEAD_DIM:(r + 1) * HEAD_DIM])

    pending = [scores(*chains[i]) for i in range(min(QK_AHEAD, len(chains)))]
    for ch in range(len(chains)):
        st = pending.pop(0)
        if ch + QK_AHEAD < len(chains):
            pending.append(scores(*chains[ch + QK_AHEAD]))
        m_prev = m_ref[ch * 8:(ch + 1) * 8, :]
        m_new = jnp.maximum(m_prev, jnp.max(st, axis=0, keepdims=True))
        alpha = jnp.exp2(m_prev - m_new)[0:1, :]
        p = jnp.exp2(st - m_new[0:1, :]).astype(vxt.dtype)
        acc_ref[ch] = alpha * acc_ref[ch] + jnp.dot(vxt, p, preferred_element_type=F32)
        m_ref[ch * 8:(ch + 1) * 8, :] = m_new

    @pl.when(kv == pl.num_programs(3) - 1)
    def _():
        for ch, (r, c) in enumerate(chains):
            a = acc_ref[ch]
            o = (a[:HEAD_DIM, :] / a[HEAD_DIM:HEAD_DIM + 1, :]).T
            o_ref[c * chain:(c + 1) * chain, r * HEAD_DIM:(r + 1) * HEAD_DIM] = o.astype(o_ref.dtype)


def _attn_b(proj3, tq=2048, tk=2048, chain=256):
    b, s, _ = proj3.shape
    qw = GRP_B * HEAD_DIM
    assert s % tq == 0 and s % tk == 0 and tq % chain == 0 and OFF_QB % qw == 0
    nchain = GRP_B * tq // chain
    return pl.pallas_call(
        functools.partial(_attn_b_kernel, chain=chain),
        grid=(b, HKV_B, s // tq, s // tk),
        in_specs=[pl.BlockSpec((None, tq, qw), lambda bi, g, qi, ki: (bi, qi, OFF_QB // qw + g)),
                  pl.BlockSpec((None, tk, HEAD_DIM), lambda bi, g, qi, ki: (bi, ki, OFF_KB // HEAD_DIM + g)),
                  pl.BlockSpec((None, tk, HEAD_DIM), lambda bi, g, qi, ki: (bi, ki, OFF_VB // HEAD_DIM + g))],
        out_specs=pl.BlockSpec((None, tq, qw), lambda bi, g, qi, ki: (bi, qi, g)),
        out_shape=jax.ShapeDtypeStruct((b, s, W_BQ), BF16),
        scratch_shapes=[pltpu.VMEM((nchain * 8, chain), F32),
                        pltpu.VMEM((nchain, HEAD_DIM + ONES_ROWS, chain), F32),
                        pltpu.VMEM((HEAD_DIM + ONES_ROWS, tk), BF16)],
        compiler_params=_params(("parallel", "parallel", "parallel", "arbitrary")),
        name="attn_axial_gqa",
    )(proj3, proj3, proj3)


def _band_bias_table():
    nt = 2 * BAND_REACH // BAND_TILE + 1
    u = np.arange(nt + 1)[:, None, None] - BAND_REACH // BAND_TILE
    rel = u * BAND_TILE + np.arange(BAND_TILE)[None, None, :] - np.arange(BAND_TILE)[None, :, None]
    mult = np.zeros(rel.shape, np.int64)
    for window, dil in DILATED_PATTERNS:
        mult += (np.abs(rel) <= window // 2) & (rel % dil == 0)
    mult[nt] = 0
    return np.where(mult > 0, np.log2(np.maximum(mult, 1)), NEG_INF).astype(np.float32)


def _attn_a_kernel(q_ref, k_ref, v_ref, tbl_ref, o_ref, vx_ref, *, nblk, wblk, heads, qsub):
    t = BAND_TILE
    r = BAND_REACH // t
    nt = 2 * r + 1
    qi = pl.program_id(2)

    @pl.when(qi == 0)
    def _():
        vx_ref[:, :, HEAD_DIM:] = jnp.ones((qsub * heads, wblk * t, HEAD_DIM), vx_ref.dtype)

    def scores(u, hh):
        qt = qi * qsub + u
        sb = jnp.clip(qt - r, 0, nblk - wblk)
        hs = slice(hh * HEAD_DIM, (hh + 1) * HEAD_DIM)
        q = q_ref[u * t:(u + 1) * t, hs]
        s = []
        for c in range(wblk):
            d = sb + c - qt + r
            rows = pl.ds(pl.multiple_of((sb + c) * t, t), t)
            vx_ref[u * heads + hh, c * t:(c + 1) * t, :HEAD_DIM] = v_ref[rows, hs]
            s.append(_qk(q, k_ref[rows, hs]) + tbl_ref[jnp.where((d >= 0) & (d < nt), d, nt)])
        return s

    chains = [(u, hh) for u in range(qsub) for hh in range(heads)]
    pending = [scores(*chains[i]) for i in range(min(BAND_AHEAD, len(chains)))]
    for ch, (u, hh) in enumerate(chains):
        s = pending.pop(0)
        if ch + BAND_AHEAD < len(chains):
            pending.append(scores(*chains[ch + BAND_AHEAD]))
        m = functools.reduce(jnp.maximum, s)
        m = functools.reduce(jnp.maximum, [m[:, c * LANES:(c + 1) * LANES] for c in range(t // LANES)])
        m = jnp.broadcast_to(jnp.max(m, axis=-1, keepdims=True), (t, LANES))
        pv = jnp.zeros((t, 2 * HEAD_DIM), F32)
        for c in range(wblk):
            p = _exp2_rows(s[c], m).astype(vx_ref.dtype)
            pv = pv + jnp.dot(p, vx_ref[ch, c * t:(c + 1) * t, :], preferred_element_type=F32)
        o_ref[u * t:(u + 1) * t, hh * HEAD_DIM:(hh + 1) * HEAD_DIM] = (
            pv[:, :HEAD_DIM] / pv[:, HEAD_DIM:]).astype(o_ref.dtype)


def _attn_a(proj3, band_tbl, heads=2, qsub=4):
    b, s, _ = proj3.shape
    t = BAND_TILE
    nblk = s // t
    wblk = min(nblk, 2 * BAND_REACH // t + 1)
    hw = heads * HEAD_DIM
    assert s % (qsub * t) == 0 and H_A % heads == 0 and OFF_KA % hw == 0 and OFF_VA % hw == 0
    kv_spec = lambda off: pl.BlockSpec((None, s, hw), lambda bi, h, qi: (bi, 0, off // hw + h),
                                       pipeline_mode=pl.Buffered(1))
    return pl.pallas_call(
        functools.partial(_attn_a_kernel, nblk=nblk, wblk=wblk, heads=heads, qsub=qsub),
        grid=(b, H_A // heads, nblk // qsub),
        in_specs=[pl.BlockSpec((None, qsub * t, hw), lambda bi, h, qi: (bi, qi, OFF_QA // hw + h)),
                  kv_spec(OFF_KA), kv_spec(OFF_VA),
                  pl.BlockSpec(band_tbl.shape, lambda bi, h, qi: (0, 0, 0), pipeline_mode=pl.Buffered(1))],
        out_specs=pl.BlockSpec((None, qsub * t, hw), lambda bi, h, qi: (bi, qi, h)),
        out_shape=jax.ShapeDtypeStruct((b, s, W_A), BF16),
        scratch_shapes=[pltpu.VMEM((qsub * heads, wblk * t, 2 * HEAD_DIM), BF16)],
        compiler_params=_params(("parallel", "parallel", "arbitrary")),
        name="attn_dilated",
    )(proj3, proj3, proj3, band_tbl)


def _rpb_table_kernel(rpb_ref, o_ref):
    h = pl.program_id(0)
    a = pl.program_id(1) - 1
    n_roff, n_coff = 2 * NA_ROWS - 1, 2 * NA_COLS - 1
    shape = (GRID_W, 2 * GRID_W)
    qc = lax.broadcasted_iota(jnp.int32, shape, 0)
    lane = lax.broadcasted_iota(jnp.int32, shape, 1)
    hi = lane >= GRID_W
    kc = jnp.where(hi, lane - GRID_W, lane)
    coff = jnp.clip(kc - qc, -(NA_COLS - 1), NA_COLS - 1) + (NA_COLS - 1)
    cstart = jnp.clip(qc - NA_COLS // 2, 0, GRID_W - NA_COLS)
    cmask = (kc >= cstart) & (kc < cstart + NA_COLS)
    a_lo = jnp.clip(a, 0, n_roff - 1)
    a_hi = jnp.clip(a + 1, 0, n_roff - 1)
    val = jnp.zeros(shape, F32)
    for bb in range(n_coff):
        r_lo = rpb_ref[(h * n_roff + a_lo) * n_coff + bb]
        r_hi = rpb_ref[(h * n_roff + a_hi) * n_coff + bb]
        val = jnp.where(coff == bb, jnp.where(hi, r_hi, r_lo), val)
    ok_lo = ((a >= 0) & (a < n_roff)).astype(jnp.int32)
    ok_hi = ((a + 1 >= 0) & (a + 1 < n_roff)).astype(jnp.int32)
    ok = jnp.where(hi, ok_hi, ok_lo) > 0
    o_ref[0, 0] = jnp.where(cmask & ok, val * LOG2_E, NEG_INF)


def _rpb_table(rpb):
    h = rpb.shape[0]
    n_slots = 2 * NA_ROWS
    return pl.pallas_call(
        _rpb_table_kernel,
        grid=(h, n_slots),
        in_specs=[pl.BlockSpec(memory_space=pltpu.SMEM)],
        out_specs=pl.BlockSpec((1, 1, GRID_W, 2 * GRID_W), lambda hi, ai: (hi, ai, 0, 0)),
        out_shape=jax.ShapeDtypeStruct((h, n_slots, GRID_W, 2 * GRID_W), F32),
        compiler_params=_params(("arbitrary", "arbitrary")),
        name="rpb_table",
    )(rpb.reshape(-1))


def _attn_c_kernel(q_ref, k_ref, v_ref, tc_ref, o_ref, vx_ref, *, rows, heads, qsub):
    nb = rows // NA_QROWS
    kblk = NA_KROWS // NA_QROWS
    nk = NA_KROWS * GRID_W

    @pl.when(pl.program_id(2) == 0)
    def _():
        vx_ref[:, :, HEAD_DIM:] = jnp.ones((qsub * heads, nk, HEAD_DIM), vx_ref.dtype)

    lane = lax.broadcasted_iota(jnp.int32, (GRID_W, 2 * GRID_W), 1)
    for u in range(qsub):
        b = pl.program_id(2) * qsub + u
        kb = jnp.clip(b - 1, 0, nb - kblk)
        win = pl.ds(pl.multiple_of(kb * NA_TILE, NA_TILE), nk)
        qrows = slice(u * NA_TILE, (u + 1) * NA_TILE)
        sel = []
        for ql in range(NA_QROWS):
            i = b * NA_QROWS + ql
            rstart = jnp.clip(i - NA_ROWS // 2, 0, rows - NA_ROWS)
            for pr in range(NA_KROWS // 2):
                kr = kb * NA_QROWS + 2 * pr
                a = kr - i + (NA_ROWS - 1)
                v_lo = ((kr >= rstart) & (kr < rstart + NA_ROWS)).astype(jnp.int32)
                v_hi = ((kr + 1 >= rstart) & (kr + 1 < rstart + NA_ROWS)).astype(jnp.int32)
                sel.append((jnp.where(lane < GRID_W, v_lo, v_hi) > 0, jnp.clip(a + 1, 0, 2 * NA_ROWS - 1)))
        for hh in range(heads):
            ch = u * heads + hh
            hs = slice(hh * HEAD_DIM, (hh + 1) * HEAD_DIM)
            vx_ref[ch, :, :HEAD_DIM] = v_ref[win, hs]
            strips = []
            for ql in range(NA_QROWS):
                pieces = []
                for pr in range(NA_KROWS // 2):
                    valid, slot = sel[ql * (NA_KROWS // 2) + pr]
                    pieces.append(jnp.where(valid, tc_ref[hh, slot], NEG_INF))
                strips.append(jnp.concatenate(pieces, axis=1))
            s = _qk(q_ref[qrows, hs], k_ref[win, hs]) + jnp.concatenate(strips, axis=0)
            m = functools.reduce(jnp.maximum, [s[:, c * LANES:(c + 1) * LANES] for c in range(nk // LANES)])
            m = jnp.broadcast_to(jnp.max(m, axis=-1, keepdims=True), (NA_TILE, LANES))
            p = _exp2_rows(s, m).astype(vx_ref.dtype)
            pv = jnp.dot(p, vx_ref[ch], preferred_element_type=F32)
            o_ref[qrows, hs] = (pv[:, :HEAD_DIM] / pv[:, HEAD_DIM:]).astype(o_ref.dtype)


def _attn_c(proj3, tc, heads=2, qsub=8):
    b, s, _ = proj3.shape
    rows = s // GRID_W
    hw = heads * HEAD_DIM
    tq = qsub * NA_TILE
    assert s % tq == 0 and rows % NA_QROWS == 0 and rows >= NA_KROWS
    assert H_C % heads == 0 and OFF_QC % hw == 0 and OFF_KC % hw == 0 and OFF_VC % hw == 0
    kv_spec = lambda off: pl.BlockSpec((None, s, hw), lambda bi, h, qi: (bi, 0, off // hw + h),
                                       pipeline_mode=pl.Buffered(1))
    return pl.pallas_call(
        functools.partial(_attn_c_kernel, rows=rows, heads=heads, qsub=qsub),
        grid=(b, H_C // heads, s // tq),
        in_specs=[pl.BlockSpec((None, tq, hw), lambda bi, h, qi: (bi, qi, OFF_QC // hw + h)),
                  kv_spec(OFF_KC), kv_spec(OFF_VC),
                  pl.BlockSpec((heads, 2 * NA_ROWS, GRID_W, 2 * GRID_W), lambda bi, h, qi: (h, 0, 0, 0))],
        out_specs=pl.BlockSpec((None, tq, hw), lambda bi, h, qi: (bi, qi, h)),
        out_shape=jax.ShapeDtypeStruct((b, s, W_C), BF16),
        scratch_shapes=[pltpu.VMEM((qsub * heads, NA_KROWS * GRID_W, 2 * HEAD_DIM), BF16)],
        compiler_params=_params(("parallel", "parallel", "arbitrary")),
        name="attn_neighbourhood",
    )(proj3, proj3, proj3, tc)


def _layer(x, xb, b, s, lw, tabs, band_tbl, last):
    m = xb.shape[0]
    proj = _in_proj(xb, lw["w_in"], tabs, lw["g_qn"], lw["g_kn"], s)
    proj3 = proj.reshape(b, s, IN_WIDTH)
    oa = _attn_a(proj3, band_tbl).reshape(m, W_A)
    ob = _attn_b(proj3).reshape(m, W_BQ)
    oc = _attn_c(proj3, lw["tc"]).reshape(m, W_C)
    y1 = _out_proj(oa, ob, oc, lw["g_grp"], lw["w_out"], x, lw["alpha"])
    xb1, mu1, rs1 = _layer_norm(y1, lw["ln1_g"], lw["ln1_b"], final=False)
    hdn = _gate_up(xb1, lw["w_gate"], lw["w_up"], tm=1024)
    y2 = _mm_resid(hdn, lw["w_down"], (y1, mu1, rs1, lw["ln1_g"], lw["ln1_b"]), lw["alpha"], tm=512)
    if last:
        return _layer_norm(y2, lw["ln2_g"], lw["ln2_b"], final=True), None
    xb2, mu2, rs2 = _layer_norm(y2, lw["ln2_g"], lw["ln2_b"], final=False)
    return (y2, mu2, rs2, lw["ln2_g"], lw["ln2_b"]), xb2


def kernel(x_prompt, x_sample, w_in, g_qn, g_kn, rpb, g_grp, w_out, ln1_g, ln1_b, w_gate, w_up, w_down,
           ln2_g, ln2_b):
    depth = w_in.shape[0]
    layers = []
    for l in range(depth):
        layers.append(dict(
            w_in=_cast_tiled(w_in, l, 512), w_out=_cast_tiled(w_out, l, 512),
            w_gate=_cast_tiled(w_gate, l, 256), w_up=_cast_tiled(w_up, l, 256),
            w_down=_cast_tiled(w_down, l, 256),
            g_qn=g_qn[l].reshape(1, HEAD_DIM), g_kn=g_kn[l].reshape(1, HEAD_DIM),
            tc=_rpb_table(rpb[l]), g_grp=g_grp[l], alpha=(2.0 * depth) ** 0.25,
            ln1_g=ln1_g[l], ln1_b=ln1_b[l], ln2_g=ln2_g[l], ln2_b=ln2_b[l]))
    band_tbl = jnp.asarray(_band_bias_table())
    tabs = _rope_tables(max(x_prompt.shape[1], x_sample.shape[1]))

    def trunk(x3):
        b, s, d = x3.shape
        x = x3.reshape(b * s, d)
        xb = x.astype(BF16)
        for l, lw in enumerate(layers):
            x, xb = _layer(x, xb, b, s, lw, tabs, band_tbl, last=l == depth - 1)
        return x.reshape(b, s, d)

    return (trunk(x_prompt), trunk(x_sample))
```
